```python
import jax, jax.numpy as jnp
from jax import lax
import numpy as np


D_MODEL = 2048
BATCH = 2
SEQ = 8192
DEPTH = 4

N_MIXERS = 3
PLE_DIM = 256
EPS = 1e-6
NEG = -1e30
BIG = 1e4

SB_HEADS = 16
SB_HEAD_DIM = D_MODEL // SB_HEADS
SB_WIDTH = SB_HEADS * SB_HEAD_DIM
SB_QBLOCK = 128

NSA_HEADS = 16
NSA_KV_GROUPS = 4
NSA_HEAD_DIM = D_MODEL // NSA_HEADS
NSA_WIDTH = NSA_HEADS * NSA_HEAD_DIM
NSA_CMP_LEN = 32
NSA_CMP_STRIDE = 16
NSA_SEL_LEN = 64
NSA_SEL_TOPK = 16
NSA_WINDOW = 512
NSA_QBLOCK = 64
NSA_N_BRANCH = 3
NSA_IN_DIM = (NSA_HEADS * NSA_HEAD_DIM + 6 * NSA_KV_GROUPS * NSA_HEAD_DIM
              + NSA_N_BRANCH * NSA_HEADS + NSA_WIDTH)

GM_GROUPS = 16
GM_CHUNK = 128
GM_WIDTH = D_MODEL
GM_GROUP_DIM = GM_WIDTH // GM_GROUPS

N_A = (DEPTH + 2) // 3
N_B = (DEPTH + 1) // 3
N_C = DEPTH // 3

kernel_name = "hybrid_sb_nsa_gmlp_decoder"


def rms_norm(x, g):
    xf = x.astype(jnp.float32)
    y = xf * lax.rsqrt(jnp.mean(xf * xf, axis=-1, keepdims=True) + EPS)
    return (y * g.astype(jnp.float32)).astype(x.dtype)


def alibi_slopes(n):
    return jnp.asarray(np.exp2(-8.0 * np.arange(1, n + 1) / n).astype(np.float32))


def stick_breaking_attention(q, k, v):
    B, S, H, Dh = q.shape
    nblk = S // SB_QBLOCK
    scale = Dh ** -0.5
    kf = k.astype(jnp.float32)
    vf = v.astype(jnp.float32)
    qb = q.reshape(B, nblk, SB_QBLOCK, H, Dh).transpose(1, 0, 2, 3, 4)
    key_pos = jnp.arange(S)

    def block(args):
        q_blk, blk = args
        t = blk * SB_QBLOCK + jnp.arange(SB_QBLOCK)
        z = jnp.einsum('bqhd,bshd->bhqs', q_blk.astype(jnp.float32), kf) * scale
        mask = key_pos[None, :] < t[:, None]
        log_1m = jnp.where(mask, jax.nn.log_sigmoid(-z), 0.0)
        tail = lax.cumsum(log_1m, axis=3, reverse=True) - log_1m
        a = jnp.where(mask, jnp.exp(jax.nn.log_sigmoid(z) + tail), 0.0)
        return jnp.einsum('bhqs,bshd->bqhd', a, vf)

    out = lax.map(block, (qb, jnp.arange(nblk)))
    return out.transpose(1, 0, 2, 3, 4).reshape(B, S, H * Dh)


def sb_mixer(h, w_in, w_out):
    B, S, _ = h.shape
    q, k, v, z = jnp.split(h @ w_in, 4, axis=-1)
    shp = (B, S, SB_HEADS, SB_HEAD_DIM)
    o = stick_breaking_attention(q.reshape(shp), k.reshape(shp), v.reshape(shp))
    return (o.astype(h.dtype) * jax.nn.silu(z)) @ w_out


def compress_blocks(kv, blk_idx, pos, w1, w2):
    B, S, G, Dh = kv.shape
    blocks = kv[:, blk_idx] + pos[:, None, :]
    n_cmp = blocks.shape[1]
    flat = blocks.transpose(0, 1, 3, 2, 4).reshape(B, n_cmp, G, NSA_CMP_LEN * Dh)
    return jax.nn.gelu(flat @ w1, approximate=False) @ w2


def nsa_mixer(h, w_in, pos_k, pos_v, ck_w1, ck_w2, cv_w1, cv_w2, w_out):
    B, S, _ = h.shape
    H, G, Dh = NSA_HEADS, NSA_KV_GROUPS, NSA_HEAD_DIM
    HG = H // G
    kvw = G * Dh
    sizes = [H * Dh, kvw, kvw, kvw, kvw, kvw, kvw, NSA_N_BRANCH * H]
    offs = np.cumsum(sizes).tolist()
    q, k_c, v_c, k_s, v_s, k_w, v_w, g, z = jnp.split(h @ w_in, offs, axis=-1)
    kv_shape = (B, S, G, Dh)

    n_cmp = (S - NSA_CMP_LEN) // NSA_CMP_STRIDE + 1
    blk_idx = np.arange(n_cmp)[:, None] * NSA_CMP_STRIDE + np.arange(NSA_CMP_LEN)[None, :]
    kc = compress_blocks(k_c.reshape(kv_shape), blk_idx, pos_k, ck_w1, ck_w2).astype(jnp.float32)
    vc = compress_blocks(v_c.reshape(kv_shape), blk_idx, pos_v, cv_w1, cv_w2).astype(jnp.float32)
    cmp_end = jnp.arange(n_cmp) * NSA_CMP_STRIDE + NSA_CMP_LEN - 1

    k_sel = k_s.reshape(kv_shape).transpose(0, 2, 1, 3).astype(jnp.float32)
    v_sel = v_s.reshape(kv_shape).transpose(0, 2, 1, 3).astype(jnp.float32)

    pad = ((0, 0), (NSA_WINDOW, 0), (0, 0), (0, 0))
    k_win = jnp.pad(k_w.reshape(kv_shape).astype(jnp.float32), pad)
    v_win = jnp.pad(v_w.reshape(kv_shape).astype(jnp.float32), pad)

    gates = jax.nn.sigmoid(g.astype(jnp.float32)).reshape(B, S, NSA_N_BRANCH, G, HG)
    slopes = alibi_slopes(H).reshape(G, HG)
    n_sel_blocks = S // NSA_SEL_LEN
    top_n = min(NSA_SEL_TOPK, n_sel_blocks)
    ratio = NSA_SEL_LEN // NSA_CMP_STRIDE
    cmp_pad = n_sel_blocks * ratio - n_cmp
    Q = NSA_QBLOCK
    n_blk = S // Q
    scale = Dh ** -0.5
    b_idx = jnp.arange(B)[:, None, None, None]
    g_idx = jnp.arange(G)[None, :, None, None]
    sel_off = jnp.arange(NSA_SEL_LEN)
    win_off = jnp.arange(NSA_WINDOW + Q)
    sel_j = jnp.arange(n_sel_blocks)

    def block(args):
        q_blk, g_blk, blk = args
        t = blk * Q + jnp.arange(Q)
        qf = q_blk.astype(jnp.float32) * scale

        dist_c = t[:, None] - cmp_end[None, :]
        mask_c = dist_c >= 0
        s_c = jnp.einsum('bqgnd,bcgd->bgnqc', qf, kc) - slopes[:, :, None, None] * dist_c.astype(jnp.float32)
        p_c = jax.nn.softmax(jnp.where(mask_c, s_c, NEG), axis=-1) * mask_c
        o_c = jnp.einsum('bgnqc,bcgd->bqgnd', p_c, vc)

        imp = jnp.pad(p_c.sum(axis=2), ((0, 0), (0, 0), (0, 0), (0, cmp_pad)))
        imp = imp.reshape(B, G, Q, n_sel_blocks, ratio).sum(-1)
        cur = (t // NSA_SEL_LEN)[:, None]
        forced = (sel_j == 0) | (sel_j == cur) | (sel_j == cur - 1)
        imp = jnp.where(forced, BIG, jnp.where(sel_j <= cur, imp, -BIG))
        _, sel = lax.top_k(imp, top_n)
        tok = (sel[..., None] * NSA_SEL_LEN + sel_off).reshape(B, G, Q, top_n * NSA_SEL_LEN)
        k_g = k_sel[b_idx, g_idx, tok]
        v_g = v_sel[b_idx, g_idx, tok]
        dist_s = t[None, None, :, None] - tok
        s_s = (jnp.einsum('bqgnd,bgqtd->bgnqt', qf, k_g)
               - slopes[None, :, :, None, None] * dist_s[:, :, None].astype(jnp.float32))
        mask_s = (dist_s >= 0)[:, :, None]
        p_s = jax.nn.softmax(jnp.where(mask_s, s_s, NEG), axis=-1)
        o_s = jnp.einsum('bgnqt,bgqtd->bqgnd', p_s, v_g)

        k_b = lax.dynamic_slice_in_dim(k_win, blk * Q, NSA_WINDOW + Q, axis=1)
        v_b = lax.dynamic_slice_in_dim(v_win, blk * Q, NSA_WINDOW + Q, axis=1)
        s_pos = blk * Q - NSA_WINDOW + win_off
        dist_w = t[:, None] - s_pos[None, :]
        mask_w = (dist_w >= 0) & (dist_w < NSA_WINDOW) & (s_pos[None, :] >= 0)
        s_w = jnp.einsum('bqgnd,bkgd->bgnqk', qf, k_b) - slopes[:, :, None, None] * dist_w.astype(jnp.float32)
        p_w = jax.nn.softmax(jnp.where(mask_w, s_w, NEG), axis=-1)
        o_w = jnp.einsum('bgnqk,bkgd->bqgnd', p_w, v_b)

        return (g_blk[:, :, 0, :, :, None] * o_c + g_blk[:, :, 1, :, :, None] * o_s
                + g_blk[:, :, 2, :, :, None] * o_w)

    qb = q.reshape(B, n_blk, Q, G, HG, Dh).transpose(1, 0, 2, 3, 4, 5)
    gb = gates.reshape(B, n_blk, Q, NSA_N_BRANCH, G, HG).transpose(1, 0, 2, 3, 4, 5)
    o = lax.map(block, (qb, gb, jnp.arange(n_blk)))
    o = o.transpose(1, 0, 2, 3, 4, 5).reshape(B, S, H * Dh)
    return (o.astype(h.dtype) * jax.nn.silu(z)) @ w_out


def gmlp_mixer(h, w_in, v_norm_g, w_s, b_s, w_out):
    B, S, _ = h.shape
    u, v, z = jnp.split(h @ w_in, 3, axis=-1)
    u = jax.nn.gelu(u, approximate=False)
    v = rms_norm(jax.nn.gelu(v, approximate=False), v_norm_g)
    n_chunk = S // GM_CHUNK
    vc = v.reshape(B, n_chunk, GM_CHUNK, GM_GROUPS, GM_GROUP_DIM)
    causal = jnp.tril(jnp.ones((GM_CHUNK, GM_CHUNK), dtype=bool))
    ws = jnp.where(causal, w_s, jnp.zeros((), w_s.dtype))
    mixed = jnp.einsum('gts,bnsgc->bntgc', ws, vc) + b_s.T[:, :, None]
    y = u * mixed.reshape(B, S, GM_WIDTH) * jax.nn.silu(z)
    return y @ w_out


def setup_inputs(seed: int = 0) -> dict:
    key = jax.random.key(seed)
    ks = jax.random.split(key, 24)
    f32 = jnp.float32

    def normal(k, shape, scale):
        return jax.random.normal(k, shape, f32) * scale

    Dh = NSA_HEAD_DIM
    return {
        "x": normal(ks[0], (BATCH, SEQ, D_MODEL), 1.0),
        "p": normal(ks[1], (DEPTH, BATCH, SEQ, PLE_DIM), 1.0),
        "norm_g": 1.0 + normal(ks[2], (DEPTH, D_MODEL), 0.02),
        "final_norm_g": 1.0 + normal(ks[3], (D_MODEL,), 0.02),
        "ple_proj": normal(ks[4], (DEPTH, PLE_DIM, D_MODEL), PLE_DIM ** -0.5),
        "ple_gate": normal(ks[5], (DEPTH, D_MODEL, D_MODEL), D_MODEL ** -0.5),
        "sb_w_in": normal(ks[6], (N_A, D_MODEL, 4 * SB_WIDTH), D_MODEL ** -0.5),
        "sb_w_out": normal(ks[7], (N_A, SB_WIDTH, D_MODEL), SB_WIDTH ** -0.5),
        "nsa_w_in": normal(ks[8], (N_B, D_MODEL, NSA_IN_DIM), D_MODEL ** -0.5),
        "nsa_cmp_pos_k": normal(ks[9], (N_B, NSA_CMP_LEN, Dh), 0.1),
        "nsa_cmp_pos_v": normal(ks[10], (N_B, NSA_CMP_LEN, Dh), 0.1),
        "nsa_cmp_k_w1": normal(ks[11], (N_B, NSA_CMP_LEN * Dh, Dh), (NSA_CMP_LEN * Dh) ** -0.5),
        "nsa_cmp_k_w2": normal(ks[12], (N_B, Dh, Dh), Dh ** -0.5),
        "nsa_cmp_v_w1": normal(ks[13], (N_B, NSA_CMP_LEN * Dh, Dh), (NSA_CMP_LEN * Dh) ** -0.5),
        "nsa_cmp_v_w2": normal(ks[14], (N_B, Dh, Dh), Dh ** -0.5),
        "nsa_w_out": normal(ks[15], (N_B, NSA_WIDTH, D_MODEL), NSA_WIDTH ** -0.5),
        "gm_w_in": normal(ks[16], (N_C, D_MODEL, 3 * GM_WIDTH), D_MODEL ** -0.5),
        "gm_v_norm_g": 1.0 + normal(ks[17], (N_C, GM_WIDTH), 0.02),
        "gm_w_s": normal(ks[18], (N_C, GM_GROUPS, GM_CHUNK, GM_CHUNK), GM_CHUNK ** -0.5),
        "gm_b_s": 1.0 + normal(ks[19], (N_C, GM_GROUPS, GM_CHUNK), 0.1),
        "gm_w_out": normal(ks[20], (N_C, GM_WIDTH, D_MODEL), GM_WIDTH ** -0.5),
    }


def reference(x, p, norm_g, final_norm_g, ple_proj, ple_gate, sb_w_in, sb_w_out,
              nsa_w_in, nsa_cmp_pos_k, nsa_cmp_pos_v, nsa_cmp_k_w1, nsa_cmp_k_w2,
              nsa_cmp_v_w1, nsa_cmp_v_w2, nsa_w_out,
              gm_w_in, gm_v_norm_g, gm_w_s, gm_b_s, gm_w_out):
    for i in range(DEPTH):
        h = rms_norm(x, norm_g[i])
        kind = i % N_MIXERS
        j = i // N_MIXERS
        if kind == 0:
            y = sb_mixer(h, sb_w_in[j], sb_w_out[j])
        elif kind == 1:
            y = nsa_mixer(h, nsa_w_in[j], nsa_cmp_pos_k[j], nsa_cmp_pos_v[j],
                          nsa_cmp_k_w1[j], nsa_cmp_k_w2[j], nsa_cmp_v_w1[j], nsa_cmp_v_w2[j],
                          nsa_w_out[j])
        else:
            y = gmlp_mixer(h, gm_w_in[j], gm_v_norm_g[j], gm_w_s[j], gm_b_s[j], gm_w_out[j])
        x = x + y.astype(x.dtype)
        x = x + jax.nn.sigmoid(x @ ple_gate[i]) * (p[i] @ ple_proj[i])
    return rms_norm(x, final_norm_g)
```

```python
import functools

import numpy as np
import jax
import jax.numpy as jnp
from jax import lax
from jax.experimental import pallas as pl
from jax.experimental.pallas import tpu as pltpu

F32 = jnp.float32
BF16 = jnp.bfloat16

EPS = 1e-6
NEG = -1e30
BIG = 1e4
LANE = 128
VMEM_LIMIT = 56 * 1024 * 1024

PLE_DIM = 256
SB_HEADS = 16
NSA_HEADS = 16
NSA_GROUPS = 4
NSA_HG = NSA_HEADS // NSA_GROUPS
NSA_CMP_LEN = 32
NSA_CMP_STRIDE = 16
NSA_SEL_LEN = 64
NSA_SEL_TOPK = 16
NSA_WINDOW = 512
NSA_BRANCHES = 3
GM_GROUPS = 16
GM_CHUNK = 128

SB_UNDERFLOW = 110.0


def _dot(a, b):
    return jnp.dot(a, b, preferred_element_type=F32)


def _dot_nt(a, b):
    return lax.dot_general(a, b, (((1,), (1,)), ((), ())), preferred_element_type=F32)


def _gelu(x):
    return 0.5 * x * (1.0 + lax.erf(x * np.float32(1.0 / np.sqrt(2.0))))


def _silu(x):
    return x * jax.nn.sigmoid(x)


def _norm_proj_kernel(x_ref, g_ref, w_ref, cs_ref, *rest, nb1, n_out):
    hn_ref = rest[-1]
    outs = rest[:n_out]
    j = pl.program_id(1)

    @pl.when(j == 0)
    def _():
        xf = x_ref[...]
        ms = jnp.mean(xf * xf, axis=-1, keepdims=True)
        hn_ref[...] = (xf * lax.rsqrt(ms + EPS) * g_ref[...]).astype(BF16)

    def compute():
        return _dot(hn_ref[...], w_ref[...]) * cs_ref[...]

    if n_out == 1:
        outs[0][...] = compute().astype(outs[0].dtype)
    else:
        @pl.when(j < nb1)
        def _():
            outs[0][...] = compute().astype(outs[0].dtype)

        @pl.when(j >= nb1)
        def _():
            outs[1][...] = compute().astype(outs[1].dtype)


def _norm_proj(x2d, g, w, col_scale, n1, *, tm=1024, tn=512):
    n, d = x2d.shape
    ntot = w.shape[1]
    n2 = ntot - n1
    tm = min(tm, n)
    assert n % tm == 0 and n1 % tn == 0 and n2 % tn == 0
    nb1 = n1 // tn
    out_shape, out_specs = [], []
    if n1:
        out_shape.append(jax.ShapeDtypeStruct((n, n1), BF16))
        out_specs.append(pl.BlockSpec((tm, tn), lambda i, j: (i, jnp.minimum(j, nb1 - 1))))
    if n2:
        out_shape.append(jax.ShapeDtypeStruct((n, n2), F32))
        out_specs.append(pl.BlockSpec((tm, tn), lambda i, j: (i, jnp.maximum(j - nb1, 0))))
    outs = pl.pallas_call(
        functools.partial(_norm_proj_kernel, nb1=nb1, n_out=len(out_shape)),
        grid=(n // tm, ntot // tn),
        in_specs=[
            pl.BlockSpec((tm, d), lambda i, j: (i, 0)),
            pl.BlockSpec((1, d), lambda i, j: (0, 0)),
            pl.BlockSpec((d, tn), lambda i, j: (0, j)),
            pl.BlockSpec((1, tn), lambda i, j: (0, j)),
        ],
        out_specs=out_specs,
        out_shape=out_shape,
        scratch_shapes=[pltpu.VMEM((tm, d), BF16)],
        compiler_params=pltpu.CompilerParams(
            dimension_semantics=("arbitrary", "arbitrary"), vmem_limit_bytes=VMEM_LIMIT),
        name="norm_proj",
    )(x2d, g.reshape(1, d), w, col_scale.reshape(1, ntot))
    outs = list(outs)
    o1 = outs.pop(0) if n1 else None
    o2 = outs.pop(0) if n2 else None
    return o1, o2


def _out_ple_kernel(y_ref, x_ref, p_ref, wo_ref, wg_ref, wp_ref, fg_ref, o_ref, *, final):
    x2 = x_ref[...] + _dot(y_ref[...], wo_ref[...])
    gate = jax.nn.sigmoid(_dot(x2.astype(BF16), wg_ref[...]))
    proj = _dot(p_ref[...].astype(BF16), wp_ref[...])
    x3 = x2 + gate * proj
    if final:
        ms = jnp.mean(x3 * x3, axis=-1, keepdims=True)
        x3 = x3 * lax.rsqrt(ms + EPS) * fg_ref[...]
    o_ref[...] = x3


def _out_ple(y, x2d, p2d, wo, wg, wp, fg, *, final, tm=512):
    n, d = x2d.shape
    pd = p2d.shape[1]
    tm = min(tm, n)
    assert n % tm == 0
    const = lambda shape: pl.BlockSpec(shape, lambda i: (0, 0), pipeline_mode=pl.Buffered(1))
    return pl.pallas_call(
        functools.partial(_out_ple_kernel, final=final),
        grid=(n // tm,),
        in_specs=[
            pl.BlockSpec((tm, d), lambda i: (i, 0)),
            pl.BlockSpec((tm, d), lambda i: (i, 0)),
            pl.BlockSpec((tm, pd), lambda i: (i, 0)),
            const((d, d)), const((d, d)), const((pd, d)), const((1, d)),
        ],
        out_specs=pl.BlockSpec((tm, d), lambda i: (i, 0)),
        out_shape=jax.ShapeDtypeStruct((n, d), F32),
        compiler_params=pltpu.CompilerParams(
            dimension_semantics=("arbitrary",), vmem_limit_bytes=VMEM_LIMIT),
        name="out_ple",
    )(y, x2d, p2d, wo, wg, wp, fg.reshape(1, d))


def _sb_kernel(q_ref, k_ref, v_ref, z_ref, o_ref, carry_ref, acc_ref, *, tq):
    i = pl.program_id(2)
    q = q_ref[0]
    rows = lax.broadcasted_iota(jnp.int32, (tq, tq), 0)
    cols = lax.broadcasted_iota(jnp.int32, (tq, tq), 1)
    later = jnp.where(rows > cols, 1.0, 0.0).astype(BF16)
    strictly_causal = cols < rows

    def tile(kt, diag):
        start = pl.multiple_of(kt * tq, tq)
        k = k_ref[0, pl.ds(start, tq), :]
        v = v_ref[0, pl.ds(start, tq), :]
        z = _dot_nt(q, k)
        l1p = jnp.log1p(jnp.exp(-jnp.abs(z)))
        sp = jnp.maximum(z, 0.0) + l1p
        lsz = jnp.minimum(z, 0.0) - l1p
        lm = jnp.where(strictly_causal, sp, 0.0) if diag else sp
        hi = lm.astype(BF16)
        lo = (lm - hi.astype(F32)).astype(BF16)
        tail = _dot(hi, later) + _dot(lo, later)
        a = jnp.exp(lsz - tail - carry_ref[...])
        if diag:
            a = jnp.where(strictly_causal, a, 0.0)
        acc_ref[...] += _dot(a.astype(BF16), v)
        carry_ref[...] += jnp.sum(lm, axis=1, keepdims=True)

    carry_ref[...] = jnp.zeros_like(carry_ref)
    acc_ref[...] = jnp.zeros_like(acc_ref)
    tile(i, True)

    def cond(st):
        kt, cmin = st
        return jnp.logical_and(kt >= 0, cmin < SB_UNDERFLOW)

    def body(st):
        kt, _ = st
        tile(kt, False)
        return kt - 1, jnp.min(carry_ref[...])

    lax.while_loop(cond, body, (i - 1, jnp.min(carry_ref[...])))
    o_ref[0] = (acc_ref[...] * _silu(z_ref[0])).astype(o_ref.dtype)


def _sb_attention(qkv, zf, b, s, *, tq=256):
    h = SB_HEADS
    tq = min(tq, s)
    assert s % tq == 0
    return pl.pallas_call(
        functools.partial(_sb_kernel, tq=tq),
        grid=(b, h, s // tq),
        in_specs=[
            pl.BlockSpec((1, tq, LANE), lambda bi, hi, i: (bi, i, hi)),
            pl.BlockSpec((1, s, LANE), lambda bi, hi, i: (bi, 0, h + hi)),
            pl.BlockSpec((1, s, LANE), lambda bi, hi, i: (bi, 0, 2 * h + hi)),
            pl.BlockSpec((1, tq, LANE), lambda bi, hi, i: (bi, i, hi)),
        ],
        out_specs=pl.BlockSpec((1, tq, LANE), lambda bi, hi, i: (bi, i, hi)),
        out_shape=jax.ShapeDtypeStruct((b, s, h * LANE), BF16),
        scratch_shapes=[pltpu.VMEM((tq, 1), F32), pltpu.VMEM((tq, LANE), F32)],
        compiler_params=pltpu.CompilerParams(
            dimension_semantics=("arbitrary", "arbitrary", "arbitrary"),
            vmem_limit_bytes=VMEM_LIMIT),
        name="sb_attention",
    )(qkv, qkv, qkv, zf)


def _cmp_kernel(x_ref, pos_ref, w1_ref, w2_ref, o_ref, *, ncb):
    half = NSA_CMP_LEN // 2
    y1 = jnp.zeros((ncb, LANE), F32)
    y2 = jnp.zeros((ncb, LANE), F32)
    for l in range(NSA_CMP_LEN):
        xl = x_ref[0, pl.ds(l % half, ncb, stride=NSA_CMP_STRIDE), :]
        xl = (xl + pos_ref[0, l:l + 1, :]).astype(BF16)
        part = _dot(xl, w1_ref[0, l * LANE:(l + 1) * LANE, :])
        if l < half:
            y1 = y1 + part
        else:
            y2 = y2 + part
    pre = y1 + pltpu.roll(y2, ncb - 1, 0)
    out = _dot(_gelu(pre).astype(BF16), w2_ref[0])
    valid = lax.broadcasted_iota(jnp.int32, (ncb, LANE), 0) < ncb - 1
    o_ref[0, 0, 0] = jnp.where(valid, out, 0.0).astype(o_ref.dtype)


def _nsa_compress(pf, pos, w1, w2, b, s, col0):
    g = NSA_GROUPS
    ncb = s // NSA_CMP_STRIDE
    return pl.pallas_call(
        functools.partial(_cmp_kernel, ncb=ncb),
        grid=(2, b, g),
        in_specs=[
            pl.BlockSpec((1, s, LANE), lambda kv, bi, gi: (bi, 0, col0 + kv * g + gi)),
            pl.BlockSpec((1, NSA_CMP_LEN, LANE), lambda kv, bi, gi: (kv, 0, 0)),
            pl.BlockSpec((1, NSA_CMP_LEN * LANE, LANE), lambda kv, bi, gi: (kv, 0, 0)),
            pl.BlockSpec((1, LANE, LANE), lambda kv, bi, gi: (kv, 0, 0)),
        ],
        out_specs=pl.BlockSpec((1, 1, 1, ncb, LANE), lambda kv, bi, gi: (kv, bi, gi, 0, 0)),
        out_shape=jax.ShapeDtypeStruct((2, b, g, ncb, LANE), BF16),
        compiler_params=pltpu.CompilerParams(
            dimension_semantics=("arbitrary", "arbitrary", "arbitrary"),
            vmem_limit_bytes=VMEM_LIMIT),
        name="nsa_compress",
    )(pf, pos, w1, w2)


def _nsa_kernel(slopes_ref, q_ref, ks_ref, vs_ref, kw_ref, vw_ref, kc_ref, vc_ref, gate_ref, z_ref,
                y_ref, m_ref, l_ref, acc_ref, *, tq, nsel):
    gi = pl.program_id(1)
    i = pl.program_id(2)
    hg = NSA_HG
    r = hg * tq
    ncb = nsel * (NSA_SEL_LEN // NSA_CMP_STRIDE)
    ratio = NSA_SEL_LEN // NSA_CMP_STRIDE
    t0 = i * tq

    qg = q_ref[0]
    qall = jnp.concatenate([qg[:, n * LANE:(n + 1) * LANE] for n in range(hg)], axis=0)
    row = lax.broadcasted_iota(jnp.int32, (r, 1), 0)
    slope = jnp.zeros((r, 1), F32)
    for n in range(hg):
        slope = jnp.where(row // tq == n, slopes_ref[gi * hg + n], slope)
    trow = t0 + row % tq

    cc = lax.broadcasted_iota(jnp.int32, (1, ncb), 1)
    cend = (ratio * (cc % nsel) + cc // nsel) * NSA_CMP_STRIDE + (NSA_CMP_LEN - 1)
    dist_c = trow - cend
    mask_c = dist_c >= 0
    s_c = _dot_nt(qall, kc_ref[0, 0])
    s_c = jnp.where(mask_c, s_c - slope * dist_c.astype(F32), NEG)
    m_c = jnp.max(s_c, axis=1, keepdims=True)
    e_c = jnp.where(mask_c, jnp.exp(s_c - m_c), 0.0)
    l_c = jnp.sum(e_c, axis=1, keepdims=True)
    p_c = e_c * jnp.where(l_c > 0.0, 1.0 / l_c, 0.0)
    o_c = _dot(p_c.astype(BF16), vc_ref[0, 0])

    ph = p_c[0:tq]
    for n in range(1, hg):
        ph = ph + p_c[n * tq:(n + 1) * tq]
    imp = ph[:, 0:nsel]
    for a in range(1, ratio):
        imp = imp + ph[:, a * nsel:(a + 1) * nsel]
    jj = lax.broadcasted_iota(jnp.int32, (tq, nsel), 1)
    cur = (t0 + lax.broadcasted_iota(jnp.int32, (tq, 1), 0)) // NSA_SEL_LEN
    forced = (jj == 0) | (jj == cur) | (jj == cur - 1)
    val = jnp.where(forced, BIG, jnp.where(jj <= cur, imp, -BIG))
    sel = jnp.zeros((tq, nsel), F32)
    for _ in range(min(NSA_SEL_TOPK, nsel)):
        vmax = jnp.max(val, axis=1, keepdims=True)
        first = jnp.min(jnp.where(val == vmax, jj, nsel), axis=1, keepdims=True)
        hit = jj == first
        sel = jnp.where(hit, 1.0, sel)
        val = jnp.where(hit, -3e38, val)
    sel_b = sel.astype(BF16)

    kpos0 = lax.broadcasted_iota(jnp.int32, (1, tq), 1)
    blk_r = lax.broadcasted_iota(jnp.int32, (nsel, tq), 0)
    blk_c = lax.broadcasted_iota(jnp.int32, (nsel, tq), 1) // NSA_SEL_LEN

    def reset():
        m_ref[...] = jnp.full_like(m_ref, NEG)
        l_ref[...] = jnp.zeros_like(l_ref)
        acc_ref[...] = jnp.zeros_like(acc_ref)

    def flash_tile(k_ref, v_ref, kt, selected):
        start = pl.multiple_of(kt * tq, tq)
        k = k_ref[0, pl.ds(start, tq), :]
        v = v_ref[0, pl.ds(start, tq), :]
        dist = trow - (start + kpos0)
        if selected:
            expand = jnp.where(blk_r == kt * (tq // NSA_SEL_LEN) + blk_c, 1.0, 0.0).astype(BF16)
            picked = _dot(sel_b, expand)
            picked = jnp.concatenate([picked] * hg, axis=0)
            mask = (picked > 0.5) & (dist >= 0)
        else:
            mask = (dist >= 0) & (dist < NSA_WINDOW)
        sc = _dot_nt(qall, k)
        sc = jnp.where(mask, sc - slope * dist.astype(F32), NEG)
        m_old = m_ref[...]
        m_new = jnp.maximum(m_old, jnp.max(sc, axis=1, keepdims=True))
        alpha = jnp.exp(m_old - m_new)
        p = jnp.where(mask, jnp.exp(sc - m_new), 0.0)
        l_ref[...] = alpha * l_ref[...] + jnp.sum(p, axis=1, keepdims=True)
        acc_ref[...] = alpha * acc_ref[...] + _dot(p.astype(BF16), v)
        m_ref[...] = m_new

    reset()

    def sel_body(kt, c):
        flash_tile(ks_ref, vs_ref, kt, True)
        return c

    lax.fori_loop(0, i + 1, sel_body, 0)
    o_s = acc_ref[...] / l_ref[...]

    reset()

    def win_body(kt, c):
        flash_tile(kw_ref, vw_ref, kt, False)
        return c

    lax.fori_loop(jnp.maximum(i - NSA_WINDOW // tq, 0), i + 1, win_body, 0)
    o_w = acc_ref[...] / l_ref[...]

    gates = jax.nn.sigmoid(gate_ref[0])
    zg = z_ref[0]
    for n in range(hg):
        rs = slice(n * tq, (n + 1) * tq)
        o = (gates[:, n:n + 1] * o_c[rs]
             + gates[:, hg + n:hg + n + 1] * o_s[rs]
             + gates[:, 2 * hg + n:2 * hg + n + 1] * o_w[rs])
        cs = slice(n * LANE, (n + 1) * LANE)
        y_ref[0, :, cs] = (o * _silu(zg[:, cs])).astype(y_ref.dtype)


def _nsa_attention(pb, pf, kvc, slopes, b, s, *, tq=128):
    g, hg = NSA_GROUPS, NSA_HG
    tq = min(tq, s)
    nsel = s // NSA_SEL_LEN
    ncb = s // NSA_CMP_STRIDE
    r = hg * tq
    qblocks = NSA_HEADS
    zblocks = NSA_HEADS
    kv_spec = lambda off: pl.BlockSpec((1, s, LANE), lambda bi, gi, i, sl: (bi, 0, qblocks + off * g + gi))
    grid_spec = pltpu.PrefetchScalarGridSpec(
        num_scalar_prefetch=1,
        grid=(b, g, s // tq),
        in_specs=[
            pl.BlockSpec((1, tq, hg * LANE), lambda bi, gi, i, sl: (bi, i, gi)),
            kv_spec(0), kv_spec(1), kv_spec(2), kv_spec(3),
            pl.BlockSpec((1, 1, ncb, LANE), lambda bi, gi, i, sl: (bi, gi, 0, 0)),
            pl.BlockSpec((1, 1, ncb, LANE), lambda bi, gi, i, sl: (bi, gi, 0, 0)),
            pl.BlockSpec((1, tq, LANE), lambda bi, gi, i, sl: (bi, i, zblocks + gi)),
            pl.BlockSpec((1, tq, hg * LANE), lambda bi, gi, i, sl: (bi, i, gi)),
        ],
        out_specs=pl.BlockSpec((1, tq, hg * LANE), lambda bi, gi, i, sl: (bi, i, gi)),
        scratch_shapes=[pltpu.VMEM((r, 1), F32), pltpu.VMEM((r, 1), F32), pltpu.VMEM((r, LANE), F32)],
    )
    return pl.pallas_call(
        functools.partial(_nsa_kernel, tq=tq, nsel=nsel),
        grid_spec=grid_spec,
        out_shape=jax.ShapeDtypeStruct((b, s, NSA_HEADS * LANE), BF16),
        compiler_params=pltpu.CompilerParams(
            dimension_semantics=("arbitrary", "arbitrary", "arbitrary"),
            vmem_limit_bytes=VMEM_LIMIT),
        name="nsa_attention",
    )(slopes, pb, pb, pb, pb, pb, kvc[0], kvc[1], pf, pf)


def _gmlp_kernel(u_ref, v_ref, z_ref, vg_ref, ws_ref, bst_ref, y_ref, *, n_chunks):
    c = GM_CHUNK
    v = _gelu(v_ref[...])
    ms = jnp.mean(v * v, axis=-1, keepdims=True)
    vn = (v * lax.rsqrt(ms + EPS) * vg_ref[...]).astype(BF16)
    rows = lax.broadcasted_iota(jnp.int32, (c, c), 0)
    cols = lax.broadcasted_iota(jnp.int32, (c, c), 1)
    causal = cols <= rows
    bst = bst_ref[...]
    for gi in range(GM_GROUPS):
        ws = jnp.where(causal, ws_ref[gi], 0.0).astype(BF16)
        cs = slice(gi * LANE, (gi + 1) * LANE)
        for ci in range(n_chunks):
            rs = slice(ci * c, (ci + 1) * c)
            mixed = _dot(ws, vn[rs, cs]) + bst[:, gi:gi + 1]
            y = _gelu(u_ref[rs, cs]) * mixed * _silu(z_ref[rs, cs])
            y_ref[rs, cs] = y.astype(y_ref.dtype)


def _gmlp(pf, vg, ws, bs, *, n_chunks=2):
    n = pf.shape[0]
    w = pf.shape[1] // 3
    tm = GM_CHUNK * n_chunks
    assert n % tm == 0 and w == GM_GROUPS * LANE
    return pl.pallas_call(
        functools.partial(_gmlp_kernel, n_chunks=n_chunks),
        grid=(n // tm,),
        in_specs=[
            pl.BlockSpec((tm, w), lambda i: (i, 0)),
            pl.BlockSpec((tm, w), lambda i: (i, 1)),
            pl.BlockSpec((tm, w), lambda i: (i, 2)),
            pl.BlockSpec((1, w), lambda i: (0, 0)),
            pl.BlockSpec((GM_GROUPS, GM_CHUNK, GM_CHUNK), lambda i: (0, 0, 0)),
            pl.BlockSpec((GM_CHUNK, GM_GROUPS), lambda i: (0, 0)),
        ],
        out_specs=pl.BlockSpec((tm, w), lambda i: (i, 0)),
        out_shape=jax.ShapeDtypeStruct((n, w), BF16),
        compiler_params=pltpu.CompilerParams(
            dimension_semantics=("arbitrary",), vmem_limit_bytes=VMEM_LIMIT),
        name="gmlp_mix",
    )(pf, pf, pf, vg.reshape(1, w), ws, bs.T)


def _sb_layer(x2d, b, s, norm_g, w_in, w_out):
    d = x2d.shape[1]
    width = w_in.shape[1] // 4
    scale = np.float32(LANE ** -0.5)
    col_scale = jnp.concatenate([jnp.full((width,), scale, F32), jnp.ones((3 * width,), F32)])
    qkv, zf = _norm_proj(x2d, norm_g, w_in.astype(BF16), col_scale, 3 * width)
    y = _sb_attention(qkv.reshape(b, s, 3 * width), zf.reshape(b, s, width), b, s)
    return y.reshape(b * s, width), w_out.astype(BF16)


def _nsa_layer(x2d, b, s, norm_g, w_in, pos_k, pos_v, ck_w1, ck_w2, cv_w1, cv_w2, w_out):
    d = x2d.shape[1]
    h, g, hg = NSA_HEADS, NSA_GROUPS, NSA_HG
    kvw = g * LANE
    qw = h * LANE
    o_kc, o_vc, o_ks, o_vs, o_kw, o_vw = (qw + a * kvw for a in range(6))
    o_g = qw + 6 * kvw
    o_z = o_g + NSA_BRANCHES * h
    wg = w_in[:, o_g:o_z].reshape(d, NSA_BRANCHES, g, hg).transpose(0, 2, 1, 3)
    wg = wg.reshape(d, g, NSA_BRANCHES * hg)
    wg = jnp.pad(wg, ((0, 0), (0, 0), (0, LANE - NSA_BRANCHES * hg))).reshape(d, g * LANE)
    w_all = jnp.concatenate([
        w_in[:, :qw], w_in[:, o_ks:o_g],
        w_in[:, o_z:], wg, w_in[:, o_kc:o_ks],
    ], axis=1).astype(BF16)
    n1 = qw + 4 * kvw
    scale = np.float32(LANE ** -0.5)
    col_scale = jnp.concatenate([jnp.full((qw,), scale, F32), jnp.ones((w_all.shape[1] - qw,), F32)])
    pb, pf = _norm_proj(x2d, norm_g, w_all, col_scale, n1)
    pf = pf.reshape(b, s, pf.shape[1])
    pb = pb.reshape(b, s, n1)

    pos = jnp.stack([pos_k, pos_v])
    w1 = jnp.stack([ck_w1, cv_w1]).astype(BF16)
    w2 = jnp.stack([ck_w2, cv_w2]).astype(BF16)
    kvc = _nsa_compress(pf, pos, w1, w2, b, s, (qw + g * LANE) // LANE)
    ratio = NSA_SEL_LEN // NSA_CMP_STRIDE
    nsel = s // NSA_SEL_LEN
    kvc = kvc.reshape(2, b, g, nsel, ratio, LANE).transpose(0, 1, 2, 4, 3, 5).reshape(2, b, g, nsel * ratio, LANE)

    slopes = jnp.asarray(np.exp2(-8.0 * np.arange(1, h + 1) / h).astype(np.float32))
    y = _nsa_attention(pb, pf, kvc, slopes, b, s)
    return y.reshape(b * s, qw), w_out.astype(BF16)


def _gmlp_layer(x2d, norm_g, w_in, v_norm_g, w_s, b_s, w_out):
    ntot = w_in.shape[1]
    _, pf = _norm_proj(x2d, norm_g, w_in.astype(BF16), jnp.ones((ntot,), F32), 0)
    y = _gmlp(pf, v_norm_g, w_s, b_s)
    return y, w_out.astype(BF16)


def kernel(x, p, norm_g, final_norm_g, ple_proj, ple_gate, sb_w_in, sb_w_out, nsa_w_in, nsa_cmp_pos_k, nsa_cmp_pos_v, nsa_cmp_k_w1, nsa_cmp_k_w2, nsa_cmp_v_w1, nsa_cmp_v_w2, nsa_w_out, gm_w_in, gm_v_norm_g, gm_w_s, gm_b_s, gm_w_out):
    b, s, d = x.shape
    depth = p.shape[0]
    x2d = x.reshape(b * s, d)
    for i in range(depth):
        kind, j = i % 3, i // 3
        if kind == 0:
            y, wo = _sb_layer(x2d, b, s, norm_g[i], sb_w_in[j], sb_w_out[j])
        elif kind == 1:
            y, wo = _nsa_layer(x2d, b, s, norm_g[i], nsa_w_in[j], nsa_cmp_pos_k[j], nsa_cmp_pos_v[j],
                               nsa_cmp_k_w1[j], nsa_cmp_k_w2[j], nsa_cmp_v_w1[j], nsa_cmp_v_w2[j],
                               nsa_w_out[j])
        else:
            y, wo = _gmlp_layer(x2d, norm_g[i], gm_w_in[j], gm_v_norm_g[j], gm_w_s[j], gm_b_s[j],
                                gm_w_out[j])
        x2d = _out_ple(y, x2d, p[i].reshape(b * s, p.shape[-1]), wo,
                       ple_gate[i].astype(BF16), ple_proj[i].astype(BF16), final_norm_g,
                       final=(i == depth - 1))
    return x2d.reshape(b, s, d)
```

```python
import functools

import ml_dtypes
import numpy as np
import jax
import jax.numpy as jnp
from jax import lax
from jax.experimental import pallas as pl
from jax.experimental.pallas import tpu as pltpu

F32 = jnp.float32
BF16 = jnp.bfloat16

EPS = 1e-6
NEG = -1e30
BIG = 1e4
LANE = 128
VMEM_LIMIT = 56 * 1024 * 1024

PLE_DIM = 256
SB_HEADS = 16
NSA_HEADS = 16
NSA_GROUPS = 4
NSA_HG = NSA_HEADS // NSA_GROUPS
NSA_CMP_LEN = 32
NSA_CMP_STRIDE = 16
NSA_SEL_LEN = 64
NSA_SEL_TOPK = 16
NSA_WINDOW = 512
NSA_BRANCHES = 3
GM_GROUPS = 16
GM_CHUNK = 128

SB_UNDERFLOW = 110.0
NSA_MASK = 2.0 ** 100
NSA_IND_LANE = LANE // 2


def _dot(a, b):
    return jnp.dot(a, b, preferred_element_type=F32)


def _dot_nt(a, b):
    return lax.dot_general(a, b, (((1,), (1,)), ((), ())), preferred_element_type=F32)


def _gelu(x):
    return 0.5 * x * (1.0 + lax.erf(x * np.float32(1.0 / np.sqrt(2.0))))


def _silu(x):
    return x * jax.nn.sigmoid(x)


def _norm_proj_kernel(x_ref, g_ref, w_ref, cs_ref, *rest, nb1, n_out):
    hn_ref = rest[-1]
    outs = rest[:n_out]
    j = pl.program_id(1)

    @pl.when(j == 0)
    def _():
        xf = x_ref[...]
        ms = jnp.mean(xf * xf, axis=-1, keepdims=True)
        hn_ref[...] = (xf * lax.rsqrt(ms + EPS) * g_ref[...]).astype(BF16)

    def compute():
        return _dot(hn_ref[...], w_ref[...]) * cs_ref[...]

    if n_out == 1:
        outs[0][...] = compute().astype(outs[0].dtype)
    else:
        @pl.when(j < nb1)
        def _():
            outs[0][...] = compute().astype(outs[0].dtype)

        @pl.when(j >= nb1)
        def _():
            outs[1][...] = compute().astype(outs[1].dtype)


def _norm_proj(x2d, g, w, col_scale, n1, *, tm=1024, tn=512):
    n, d = x2d.shape
    ntot = w.shape[1]
    n2 = ntot - n1
    tm = min(tm, n)
    assert n % tm == 0 and n1 % tn == 0 and n2 % tn == 0
    nb1 = n1 // tn
    out_shape, out_specs = [], []
    if n1:
        out_shape.append(jax.ShapeDtypeStruct((n, n1), BF16))
        out_specs.append(pl.BlockSpec((tm, tn), lambda i, j: (i, jnp.minimum(j, nb1 - 1))))
    if n2:
        out_shape.append(jax.ShapeDtypeStruct((n, n2), F32))
        out_specs.append(pl.BlockSpec((tm, tn), lambda i, j: (i, jnp.maximum(j - nb1, 0))))
    outs = pl.pallas_call(
        functools.partial(_norm_proj_kernel, nb1=nb1, n_out=len(out_shape)),
        grid=(n // tm, ntot // tn),
        in_specs=[
            pl.BlockSpec((tm, d), lambda i, j: (i, 0)),
            pl.BlockSpec((1, d), lambda i, j: (0, 0)),
            pl.BlockSpec((d, tn), lambda i, j: (0, j)),
            pl.BlockSpec((1, tn), lambda i, j: (0, j)),
        ],
        out_specs=out_specs,
        out_shape=out_shape,
        scratch_shapes=[pltpu.VMEM((tm, d), BF16)],
        compiler_params=pltpu.CompilerParams(
            dimension_semantics=("arbitrary", "arbitrary"), vmem_limit_bytes=VMEM_LIMIT),
        name="norm_proj",
    )(x2d, g.reshape(1, d), w, col_scale.reshape(1, ntot))
    outs = list(outs)
    o1 = outs.pop(0) if n1 else None
    o2 = outs.pop(0) if n2 else None
    return o1, o2


def _out_ple_kernel(y_ref, x_ref, p_ref, wo_ref, wg_ref, wp_ref, fg_ref, o_ref, *, final):
    x2 = x_ref[...] + _dot(y_ref[...], wo_ref[...])
    gate = jax.nn.sigmoid(_dot(x2.astype(BF16), wg_ref[...]))
    proj = _dot(p_ref[...].astype(BF16), wp_ref[...])
    x3 = x2 + gate * proj
    if final:
        ms = jnp.mean(x3 * x3, axis=-1, keepdims=True)
        x3 = x3 * lax.rsqrt(ms + EPS) * fg_ref[...]
    o_ref[...] = x3


def _out_ple(y, x2d, p2d, wo, wg, wp, fg, *, final, tm=512):
    n, d = x2d.shape
    pd = p2d.shape[1]
    tm = min(tm, n)
    assert n % tm == 0
    const = lambda shape: pl.BlockSpec(shape, lambda i: (0, 0), pipeline_mode=pl.Buffered(1))
    return pl.pallas_call(
        functools.partial(_out_ple_kernel, final=final),
        grid=(n // tm,),
        in_specs=[
            pl.BlockSpec((tm, d), lambda i: (i, 0)),
            pl.BlockSpec((tm, d), lambda i: (i, 0)),
            pl.BlockSpec((tm, pd), lambda i: (i, 0)),
            const((d, d)), const((d, d)), const((pd, d)), const((1, d)),
        ],
        out_specs=pl.BlockSpec((tm, d), lambda i: (i, 0)),
        out_shape=jax.ShapeDtypeStruct((n, d), F32),
        compiler_params=pltpu.CompilerParams(
            dimension_semantics=("arbitrary",), vmem_limit_bytes=VMEM_LIMIT),
        name="out_ple",
    )(y, x2d, p2d, wo, wg, wp, fg.reshape(1, d))


def _sb_kernel(q_ref, k_ref, v_ref, z_ref, o_ref, carry_ref, acc_ref, *, tq):
    i = pl.program_id(2)
    q = q_ref[0]
    rows = lax.broadcasted_iota(jnp.int32, (tq, tq), 0)
    cols = lax.broadcasted_iota(jnp.int32, (tq, tq), 1)
    later = jnp.where(rows > cols, 1.0, 0.0).astype(BF16)
    strictly_causal = cols < rows

    def tile(kt, diag):
        start = pl.multiple_of(kt * tq, tq)
        k = k_ref[0, pl.ds(start, tq), :]
        v = v_ref[0, pl.ds(start, tq), :]
        z = _dot_nt(q, k)
        l1p = jnp.log1p(jnp.exp(-jnp.abs(z)))
        sp = jnp.maximum(z, 0.0) + l1p
        lsz = jnp.minimum(z, 0.0) - l1p
        lm = jnp.where(strictly_causal, sp, 0.0) if diag else sp
        hi = lm.astype(BF16)
        lo = (lm - hi.astype(F32)).astype(BF16)
        tail = _dot(hi, later) + _dot(lo, later)
        a = jnp.exp(lsz - tail - carry_ref[...])
        if diag:
            a = jnp.where(strictly_causal, a, 0.0)
        acc_ref[...] += _dot(a.astype(BF16), v)
        carry_ref[...] += jnp.sum(lm, axis=1, keepdims=True)

    carry_ref[...] = jnp.zeros_like(carry_ref)
    acc_ref[...] = jnp.zeros_like(acc_ref)
    tile(i, True)

    def cond(st):
        kt, cmin = st
        return jnp.logical_and(kt >= 0, cmin < SB_UNDERFLOW)

    def body(st):
        kt, _ = st
        tile(kt, False)
        return kt - 1, jnp.min(carry_ref[...])

    lax.while_loop(cond, body, (i - 1, jnp.min(carry_ref[...])))
    o_ref[0] = (acc_ref[...] * _silu(z_ref[0])).astype(o_ref.dtype)


def _sb_attention(qkv, zf, b, s, *, tq=256):
    h = SB_HEADS
    tq = min(tq, s)
    assert s % tq == 0
    return pl.pallas_call(
        functools.partial(_sb_kernel, tq=tq),
        grid=(b, h, s // tq),
        in_specs=[
            pl.BlockSpec((1, tq, LANE), lambda bi, hi, i: (bi, i, hi)),
            pl.BlockSpec((1, s, LANE), lambda bi, hi, i: (bi, 0, h + hi)),
            pl.BlockSpec((1, s, LANE), lambda bi, hi, i: (bi, 0, 2 * h + hi)),
            pl.BlockSpec((1, tq, LANE), lambda bi, hi, i: (bi, i, hi)),
        ],
        out_specs=pl.BlockSpec((1, tq, LANE), lambda bi, hi, i: (bi, i, hi)),
        out_shape=jax.ShapeDtypeStruct((b, s, h * LANE), BF16),
        scratch_shapes=[pltpu.VMEM((tq, 1), F32), pltpu.VMEM((tq, LANE), F32)],
        compiler_params=pltpu.CompilerParams(
            dimension_semantics=("arbitrary", "arbitrary", "arbitrary"),
            vmem_limit_bytes=VMEM_LIMIT),
        name="sb_attention",
    )(qkv, qkv, qkv, zf)


def _cmp_kernel(x_ref, pos_ref, w1_ref, w2_ref, o_ref, *, ncb):
    half = NSA_CMP_LEN // 2
    y1 = jnp.zeros((ncb, LANE), F32)
    y2 = jnp.zeros((ncb, LANE), F32)
    for l in range(NSA_CMP_LEN):
        xl = x_ref[0, pl.ds(l % half, ncb, stride=NSA_CMP_STRIDE), :]
        xl = (xl + pos_ref[0, l:l + 1, :]).astype(BF16)
        part = _dot(xl, w1_ref[0, l * LANE:(l + 1) * LANE, :])
        if l < half:
            y1 = y1 + part
        else:
            y2 = y2 + part
    pre = y1 + pltpu.roll(y2, ncb - 1, 0)
    out = _dot(_gelu(pre).astype(BF16), w2_ref[0])
    valid = lax.broadcasted_iota(jnp.int32, (ncb, LANE), 0) < ncb - 1
    o_ref[0, 0, 0] = jnp.where(valid, out, 0.0).astype(o_ref.dtype)


def _nsa_compress(pf, pos, w1, w2, b, s, col0):
    g = NSA_GROUPS
    ncb = s // NSA_CMP_STRIDE
    return pl.pallas_call(
        functools.partial(_cmp_kernel, ncb=ncb),
        grid=(2, b, g),
        in_specs=[
            pl.BlockSpec((1, s, LANE), lambda kv, bi, gi: (bi, 0, col0 + kv * g + gi)),
            pl.BlockSpec((1, NSA_CMP_LEN, LANE), lambda kv, bi, gi: (kv, 0, 0)),
            pl.BlockSpec((1, NSA_CMP_LEN * LANE, LANE), lambda kv, bi, gi: (kv, 0, 0)),
            pl.BlockSpec((1, LANE, LANE), lambda kv, bi, gi: (kv, 0, 0)),
        ],
        out_specs=pl.BlockSpec((1, 1, 1, ncb, LANE), lambda kv, bi, gi: (kv, bi, gi, 0, 0)),
        out_shape=jax.ShapeDtypeStruct((2, b, g, ncb, LANE), BF16),
        compiler_params=pltpu.CompilerParams(
            dimension_semantics=("arbitrary", "arbitrary", "arbitrary"),
            vmem_limit_bytes=VMEM_LIMIT),
        name="nsa_compress",
    )(pf, pos, w1, w2)


def _split3(x):
    x = np.asarray(x, np.float32)
    bf = ml_dtypes.bfloat16
    hi = x.astype(bf).astype(np.float32)
    mid = (x - hi).astype(bf).astype(np.float32)
    lo = (x - hi - mid).astype(bf).astype(np.float32)
    return hi, mid, lo


def _nsa_constants(s, tq, tk):
    h, g, hg = NSA_HEADS, NSA_GROUPS, NSA_HG
    bf = ml_dtypes.bfloat16
    slopes = np.exp2(-8.0 * np.arange(1, h + 1) / h).astype(np.float32)
    qbase = np.zeros((g, hg * tq, LANE), np.float32)
    for gi in range(g):
        for n in range(hg):
            pieces = np.stack(_split3(slopes[gi * hg + n]))
            qbase[gi, n * tq:(n + 1) * tq, 0:3] = pieces
            qbase[gi, n * tq:(n + 1) * tq, 3:6] = pieces
    pos = np.arange(s)
    kaug = np.zeros((s, LANE), np.float32)
    kaug[:, 0:3] = (pos % tk)[:, None]
    kaug[:, 3:6] = (pos - pos % tk)[:, None]
    kaug[pos, NSA_IND_LANE + (pos // NSA_SEL_LEN) % NSA_IND_LANE] = 1.0
    ratio = NSA_SEL_LEN // NSA_CMP_STRIDE
    nsel = s // NSA_SEL_LEN
    cc = np.arange(s // NSA_CMP_STRIDE)
    cend = (ratio * (cc % nsel) + cc // nsel) * NSA_CMP_STRIDE + (NSA_CMP_LEN - 1)
    caug = np.zeros((cc.size, LANE), np.float32)
    caug[:, 0:3] = (cend % tk)[:, None]
    caug[:, 3:6] = (cend - cend % tk)[:, None]
    return (jnp.asarray(qbase.astype(bf)), jnp.asarray(kaug.astype(bf)), jnp.asarray(caug.astype(bf)))


def _nsa_kernel(q_ref, ks_ref, vs_ref, kw_ref, vw_ref, kaug_ref, kc_ref, vc_ref, caug_ref, qbase_ref,
                gate_ref, z_ref, y_ref, qp_ref, m_ref, l_ref, acc_ref, flag_ref, *, tq, tk, nsel):
    i = pl.program_id(2)
    hg = NSA_HG
    r = hg * tq
    ratio = NSA_SEL_LEN // NSA_CMP_STRIDE
    ncb = nsel * ratio
    bpt = tk // NSA_SEL_LEN
    n_tiles = nsel // bpt
    t0 = i * tq

    qg = q_ref[0]
    qall = jnp.concatenate([qg[:, n * LANE:(n + 1) * LANE] for n in range(hg)], axis=0)
    base = qbase_ref[0]
    qp_ref[2] = jnp.concatenate([qall, base], axis=1)
    trow = t0 + lax.broadcasted_iota(jnp.int32, (r, 1), 0) % tq

    cc = lax.broadcasted_iota(jnp.int32, (1, ncb), 1)
    cend = (ratio * (cc % nsel) + cc // nsel) * NSA_CMP_STRIDE + (NSA_CMP_LEN - 1)
    mask_c = trow >= cend
    kcp = jnp.concatenate([kc_ref[0, 0], caug_ref[...]], axis=1)
    s_c = jnp.where(mask_c, _dot_nt(qp_ref[2], kcp), NEG)
    m_c = jnp.max(s_c, axis=1, keepdims=True)
    e_c = jnp.where(mask_c, jnp.exp(s_c - m_c), 0.0)
    l_c = jnp.sum(e_c, axis=1, keepdims=True)
    p_c = e_c * jnp.where(l_c > 0.0, 1.0 / l_c, 0.0)
    o_c = _dot(p_c.astype(BF16), vc_ref[0, 0])

    ph = p_c[0:tq]
    for n in range(1, hg):
        ph = ph + p_c[n * tq:(n + 1) * tq]
    imp = ph[:, 0:nsel]
    for a in range(1, ratio):
        imp = imp + ph[:, a * nsel:(a + 1) * nsel]
    if nsel < LANE:
        imp = jnp.concatenate([imp, jnp.zeros((tq, LANE - nsel), F32)], axis=1)

    imp_t = imp.T
    jj = lax.broadcasted_iota(jnp.int32, (LANE, tq), 0)
    cur = (t0 + lax.broadcasted_iota(jnp.int32, (1, tq), 1)) // NSA_SEL_LEN
    forced = (jj == 0) | (jj == cur) | (jj == cur - 1)
    val = jnp.where(forced, BIG, jnp.where(jj <= cur, imp_t, -BIG))
    sel_t = jnp.zeros((LANE, tq), F32)
    for _ in range(min(NSA_SEL_TOPK, nsel)):
        vmax = jnp.max(val, axis=0, keepdims=True)
        first = jnp.min(jnp.where(val == vmax, jj, LANE), axis=0, keepdims=True)
        hit = jj == first
        sel_t = jnp.where(hit, 1.0, sel_t)
        val = jnp.where(hit, -3e38, val)
    sel = sel_t.T

    col_any = jnp.max(sel, axis=0, keepdims=True)
    for kt in range(n_tiles):
        flag_ref[kt] = (jnp.max(col_any[:, kt * bpt:(kt + 1) * bpt]) > 0.5).astype(jnp.int32)

    lane_q = lax.broadcasted_iota(jnp.int32, (tq, LANE), 1)
    lane_r = lax.broadcasted_iota(jnp.int32, (r, LANE), 1)
    unpicked = jnp.where(sel > 0.5, 0.0, -NSA_MASK)
    for half, part in enumerate((pltpu.roll(unpicked, NSA_IND_LANE, 1), unpicked)):
        part = jnp.where(lane_q >= NSA_IND_LANE, part, 0.0).astype(BF16)
        aug = jnp.where(lane_r >= NSA_IND_LANE, jnp.concatenate([part] * hg, axis=0), base)
        qp_ref[half] = jnp.concatenate([qall, aug], axis=1)

    dist0 = (lax.broadcasted_iota(jnp.int32, (r, tk), 0) % tq
             - lax.broadcasted_iota(jnp.int32, (r, tk), 1))

    def reset():
        m_ref[...] = jnp.full_like(m_ref, NEG)
        l_ref[...] = jnp.zeros_like(l_ref)
        acc_ref[...] = jnp.zeros_like(acc_ref)

    def flash_tile(k_ref, v_ref, kt, qslot, mode):
        start = pl.multiple_of(kt * tk, tk)
        kp = jnp.concatenate([k_ref[0, pl.ds(start, tk), :], kaug_ref[pl.ds(start, tk), :]], axis=1)
        sc = _dot_nt(qp_ref[qslot], kp)
        if mode != "plain":
            dist = dist0 - (start - t0)
            mask = dist >= 0
            if mode == "window":
                mask = mask & (dist < NSA_WINDOW)
            sc = jnp.where(mask, sc, NEG)
        chunks = [sc[:, c * LANE:(c + 1) * LANE] for c in range(tk // LANE)]
        cmax = chunks[0]
        for ch in chunks[1:]:
            cmax = jnp.maximum(cmax, ch)
        m_old = m_ref[...]
        m_new = jnp.maximum(m_old, jnp.max(cmax, axis=1, keepdims=True))
        alpha = jnp.exp(m_old - m_new)
        ps = [jnp.exp(ch - m_new) for ch in chunks]
        psum = ps[0]
        for x in ps[1:]:
            psum = psum + x
        l_ref[...] = alpha * l_ref[...] + jnp.sum(psum, axis=1, keepdims=True)
        p = jnp.concatenate([x.astype(BF16) for x in ps], axis=1)
        acc_ref[...] = alpha * acc_ref[...] + _dot(p, v_ref[0, pl.ds(start, tk), :])
        m_ref[...] = m_new

    last = (t0 + tq - 1) // tk
    halves_per_tile = NSA_IND_LANE // bpt

    reset()

    def sel_body(kt, c):
        @pl.when(flag_ref[kt] > 0)
        def _():
            flash_tile(ks_ref, vs_ref, kt, kt // halves_per_tile, "plain")
        return c

    lax.fori_loop(0, last, sel_body, 0)
    flash_tile(ks_ref, vs_ref, last, last // halves_per_tile, "causal")
    o_s = acc_ref[...] / l_ref[...]

    reset()

    def win_body(kt, c):
        flash_tile(kw_ref, vw_ref, kt, 2, "window")
        return c

    lax.fori_loop(jnp.maximum(t0 - (NSA_WINDOW - 1), 0) // tk, last + 1, win_body, 0)
    o_w = acc_ref[...] / l_ref[...]

    gates = jax.nn.sigmoid(gate_ref[0])
    zg = z_ref[0]
    for n in range(hg):
        rs = slice(n * tq, (n + 1) * tq)
        o = (gates[:, n:n + 1] * o_c[rs]
             + gates[:, hg + n:hg + n + 1] * o_s[rs]
             + gates[:, 2 * hg + n:2 * hg + n + 1] * o_w[rs])
        cs = slice(n * LANE, (n + 1) * LANE)
        y_ref[0, :, cs] = (o * _silu(zg[:, cs])).astype(y_ref.dtype)


def _nsa_attention(pb, pf, kvc, b, s, *, tq=128, tk=256):
    g, hg = NSA_GROUPS, NSA_HG
    tq = min(tq, s)
    tk = min(tk, s)
    nsel = s // NSA_SEL_LEN
    ncb = s // NSA_CMP_STRIDE
    r = hg * tq
    assert s % tk == 0 and tk % tq == 0 and tk % LANE == 0 and nsel <= LANE
    assert NSA_IND_LANE % (tk // NSA_SEL_LEN) == 0 and tk <= 256
    qbase, kaug, caug = _nsa_constants(s, tq, tk)
    qblocks = NSA_HEADS
    zblocks = NSA_HEADS
    kv_spec = lambda off: pl.BlockSpec((1, s, LANE), lambda bi, gi, i: (bi, 0, qblocks + off * g + gi))
    return pl.pallas_call(
        functools.partial(_nsa_kernel, tq=tq, tk=tk, nsel=nsel),
        grid=(b, g, s // tq),
        in_specs=[
            pl.BlockSpec((1, tq, hg * LANE), lambda bi, gi, i: (bi, i, gi)),
            kv_spec(0), kv_spec(1), kv_spec(2), kv_spec(3),
            pl.BlockSpec((s, LANE), lambda bi, gi, i: (0, 0)),
            pl.BlockSpec((1, 1, ncb, LANE), lambda bi, gi, i: (bi, gi, 0, 0)),
            pl.BlockSpec((1, 1, ncb, LANE), lambda bi, gi, i: (bi, gi, 0, 0)),
            pl.BlockSpec((ncb, LANE), lambda bi, gi, i: (0, 0)),
            pl.BlockSpec((1, r, LANE), lambda bi, gi, i: (gi, 0, 0)),
            pl.BlockSpec((1, tq, LANE), lambda bi, gi, i: (bi, i, zblocks + gi)),
            pl.BlockSpec((1, tq, hg * LANE), lambda bi, gi, i: (bi, i, gi)),
        ],
        out_specs=pl.BlockSpec((1, tq, hg * LANE), lambda bi, gi, i: (bi, i, gi)),
        out_shape=jax.ShapeDtypeStruct((b, s, NSA_HEADS * LANE), BF16),
        scratch_shapes=[
            pltpu.VMEM((3, r, 2 * LANE), BF16),
            pltpu.VMEM((r, LANE), F32), pltpu.VMEM((r, LANE), F32), pltpu.VMEM((r, LANE), F32),
            pltpu.SMEM((nsel // (tk // NSA_SEL_LEN),), jnp.int32),
        ],
        compiler_params=pltpu.CompilerParams(
            dimension_semantics=("arbitrary", "arbitrary", "arbitrary"),
            vmem_limit_bytes=VMEM_LIMIT),
        name="nsa_attention",
    )(pb, pb, pb, pb, pb, kaug, kvc[0], kvc[1], caug, qbase, pf, pf)


def _gmlp_kernel(u_ref, v_ref, z_ref, vg_ref, ws_ref, bst_ref, y_ref, *, n_chunks):
    c = GM_CHUNK
    v = _gelu(v_ref[...])
    ms = jnp.mean(v * v, axis=-1, keepdims=True)
    vn = (v * lax.rsqrt(ms + EPS) * vg_ref[...]).astype(BF16)
    rows = lax.broadcasted_iota(jnp.int32, (c, c), 0)
    cols = lax.broadcasted_iota(jnp.int32, (c, c), 1)
    causal = cols <= rows
    bst = bst_ref[...]
    for gi in range(GM_GROUPS):
        ws = jnp.where(causal, ws_ref[gi], 0.0).astype(BF16)
        cs = slice(gi * LANE, (gi + 1) * LANE)
        for ci in range(n_chunks):
            rs = slice(ci * c, (ci + 1) * c)
            mixed = _dot(ws, vn[rs, cs]) + bst[:, gi:gi + 1]
            y = _gelu(u_ref[rs, cs]) * mixed * _silu(z_ref[rs, cs])
            y_ref[rs, cs] = y.astype(y_ref.dtype)


def _gmlp(pf, vg, ws, bs, *, n_chunks=2):
    n = pf.shape[0]
    w = pf.shape[1] // 3
    tm = GM_CHUNK * n_chunks
    assert n % tm == 0 and w == GM_GROUPS * LANE
    return pl.pallas_call(
        functools.partial(_gmlp_kernel, n_chunks=n_chunks),
        grid=(n // tm,),
        in_specs=[
            pl.BlockSpec((tm, w), lambda i: (i, 0)),
            pl.BlockSpec((tm, w), lambda i: (i, 1)),
            pl.BlockSpec((tm, w), lambda i: (i, 2)),
            pl.BlockSpec((1, w), lambda i: (0, 0)),
            pl.BlockSpec((GM_GROUPS, GM_CHUNK, GM_CHUNK), lambda i: (0, 0, 0)),
            pl.BlockSpec((GM_CHUNK, GM_GROUPS), lambda i: (0, 0)),
        ],
        out_specs=pl.BlockSpec((tm, w), lambda i: (i, 0)),
        out_shape=jax.ShapeDtypeStruct((n, w), BF16),
        compiler_params=pltpu.CompilerParams(
            dimension_semantics=("arbitrary",), vmem_limit_bytes=VMEM_LIMIT),
        name="gmlp_mix",
    )(pf, pf, pf, vg.reshape(1, w), ws, bs.T)


def _sb_layer(x2d, b, s, norm_g, w_in, w_out):
    width = w_in.shape[1] // 4
    scale = np.float32(LANE ** -0.5)
    col_scale = jnp.concatenate([jnp.full((width,), scale, F32), jnp.ones((3 * width,), F32)])
    qkv, zf = _norm_proj(x2d, norm_g, w_in.astype(BF16), col_scale, 3 * width)
    y = _sb_attention(qkv.reshape(b, s, 3 * width), zf.reshape(b, s, width), b, s)
    return y.reshape(b * s, width), w_out.astype(BF16)


def _nsa_layer(x2d, b, s, norm_g, w_in, pos_k, pos_v, ck_w1, ck_w2, cv_w1, cv_w2, w_out):
    d = x2d.shape[1]
    h, g, hg = NSA_HEADS, NSA_GROUPS, NSA_HG
    kvw = g * LANE
    qw = h * LANE
    o_kc, o_vc, o_ks, o_vs, o_kw, o_vw = (qw + a * kvw for a in range(6))
    o_g = qw + 6 * kvw
    o_z = o_g + NSA_BRANCHES * h
    wg = w_in[:, o_g:o_z].reshape(d, NSA_BRANCHES, g, hg).transpose(0, 2, 1, 3)
    wg = wg.reshape(d, g, NSA_BRANCHES * hg)
    wg = jnp.pad(wg, ((0, 0), (0, 0), (0, LANE - NSA_BRANCHES * hg))).reshape(d, g * LANE)
    w_all = jnp.concatenate([
        w_in[:, :qw], w_in[:, o_ks:o_g],
        w_in[:, o_z:], wg, w_in[:, o_kc:o_ks],
    ], axis=1).astype(BF16)
    n1 = qw + 4 * kvw
    scale = np.float32(LANE ** -0.5)
    col_scale = jnp.concatenate([jnp.full((qw,), scale, F32), jnp.ones((w_all.shape[1] - qw,), F32)])
    pb, pf = _norm_proj(x2d, norm_g, w_all, col_scale, n1)
    pf = pf.reshape(b, s, pf.shape[1])
    pb = pb.reshape(b, s, n1)

    pos = jnp.stack([pos_k, pos_v])
    w1 = jnp.stack([ck_w1, cv_w1]).astype(BF16)
    w2 = jnp.stack([ck_w2, cv_w2]).astype(BF16)
    kvc = _nsa_compress(pf, pos, w1, w2, b, s, (qw + g * LANE) // LANE)
    ratio = NSA_SEL_LEN // NSA_CMP_STRIDE
    nsel = s // NSA_SEL_LEN
    kvc = kvc.reshape(2, b, g, nsel, ratio, LANE).transpose(0, 1, 2, 4, 3, 5).reshape(2, b, g, nsel * ratio, LANE)

    y = _nsa_attention(pb, pf, kvc, b, s)
    return y.reshape(b * s, qw), w_out.astype(BF16)


def _gmlp_layer(x2d, norm_g, w_in, v_norm_g, w_s, b_s, w_out):
    ntot = w_in.shape[1]
    _, pf = _norm_proj(x2d, norm_g, w_in.astype(BF16), jnp.ones((ntot,), F32), 0)
    y = _gmlp(pf, v_norm_g, w_s, b_s)
    return y, w_out.astype(BF16)


def kernel(x, p, norm_g, final_norm_g, ple_proj, ple_gate, sb_w_in, sb_w_out, nsa_w_in, nsa_cmp_pos_k, nsa_cmp_pos_v, nsa_cmp_k_w1, nsa_cmp_k_w2, nsa_cmp_v_w1, nsa_cmp_v_w2, nsa_w_out, gm_w_in, gm_v_norm_g, gm_w_s, gm_b_s, gm_w_out):
    b, s, d = x.shape
    depth = p.shape[0]
    x2d = x.reshape(b * s, d)
    for i in range(depth):
        kind, j = i % 3, i // 3
        if kind == 0:
            y, wo = _sb_layer(x2d, b, s, norm_g[i], sb_w_in[j], sb_w_out[j])
        elif kind == 1:
            y, wo = _nsa_layer(x2d, b, s, norm_g[i], nsa_w_in[j], nsa_cmp_pos_k[j], nsa_cmp_pos_v[j],
                               nsa_cmp_k_w1[j], nsa_cmp_k_w2[j], nsa_cmp_v_w1[j], nsa_cmp_v_w2[j],
                               nsa_w_out[j])
        else:
            y, wo = _gmlp_layer(x2d, norm_g[i], gm_w_in[j], gm_v_norm_g[j], gm_w_s[j], gm_b_s[j],
                                gm_w_out[j])
        x2d = _out_ple(y, x2d, p[i].reshape(b * s, p.shape[-1]), wo,
                       ple_gate[i].astype(BF16), ple_proj[i].astype(BF16), final_norm_g,
                       final=(i == depth - 1))
    return x2d.reshape(b, s, d)
```

```python
import functools

import ml_dtypes
import numpy as np
import jax
import jax.numpy as jnp
from jax import lax
from jax.experimental import pallas as pl
from jax.experimental.pallas import tpu as pltpu

F32 = jnp.float32
BF16 = jnp.bfloat16

EPS = 1e-6
NEG = -1e30
BIG = 1e4
LANE = 128
VMEM_LIMIT = 56 * 1024 * 1024

PLE_DIM = 256
SB_HEADS = 16
NSA_HEADS = 16
NSA_GROUPS = 4
NSA_HG = NSA_HEADS // NSA_GROUPS
NSA_CMP_LEN = 32
NSA_CMP_STRIDE = 16
NSA_SEL_LEN = 64
NSA_SEL_TOPK = 16
NSA_WINDOW = 512
NSA_BRANCHES = 3
GM_GROUPS = 16
GM_CHUNK = 128

SB_UNDERFLOW = 110.0
NSA_MASK = 2.0 ** 100
NSA_IND_LANE = LANE // 2


def _dot(a, b):
    return jnp.dot(a, b, preferred_element_type=F32)


def _dot_nt(a, b):
    return lax.dot_general(a, b, (((1,), (1,)), ((), ())), preferred_element_type=F32)


def _gelu(x):
    return 0.5 * x * (1.0 + lax.erf(x * np.float32(1.0 / np.sqrt(2.0))))


def _silu(x):
    return x * jax.nn.sigmoid(x)


def _norm_proj_kernel(x_ref, g_ref, w_ref, cs_ref, *rest, nb1, n_out):
    hn_ref = rest[-1]
    outs = rest[:n_out]
    j = pl.program_id(1)

    @pl.when(j == 0)
    def _():
        xf = x_ref[...]
        ms = jnp.mean(xf * xf, axis=-1, keepdims=True)
        hn_ref[...] = (xf * lax.rsqrt(ms + EPS) * g_ref[...]).astype(BF16)

    def compute():
        return _dot(hn_ref[...], w_ref[...]) * cs_ref[...]

    if n_out == 1:
        outs[0][...] = compute().astype(outs[0].dtype)
    else:
        @pl.when(j < nb1)
        def _():
            outs[0][...] = compute().astype(outs[0].dtype)

        @pl.when(j >= nb1)
        def _():
            outs[1][...] = compute().astype(outs[1].dtype)


def _norm_proj(x2d, g, w, col_scale, n1, *, tm=1024, tn=512):
    n, d = x2d.shape
    ntot = w.shape[1]
    n2 = ntot - n1
    tm = min(tm, n)
    assert n % tm == 0 and n1 % tn == 0 and n2 % tn == 0
    nb1 = n1 // tn
    out_shape, out_specs = [], []
    if n1:
        out_shape.append(jax.ShapeDtypeStruct((n, n1), BF16))
        out_specs.append(pl.BlockSpec((tm, tn), lambda i, j: (i, jnp.minimum(j, nb1 - 1))))
    if n2:
        out_shape.append(jax.ShapeDtypeStruct((n, n2), F32))
        out_specs.append(pl.BlockSpec((tm, tn), lambda i, j: (i, jnp.maximum(j - nb1, 0))))
    outs = pl.pallas_call(
        functools.partial(_norm_proj_kernel, nb1=nb1, n_out=len(out_shape)),
        grid=(n // tm, ntot // tn),
        in_specs=[
            pl.BlockSpec((tm, d), lambda i, j: (i, 0)),
            pl.BlockSpec((1, d), lambda i, j: (0, 0)),
            pl.BlockSpec((d, tn), lambda i, j: (0, j)),
            pl.BlockSpec((1, tn), lambda i, j: (0, j)),
        ],
        out_specs=out_specs,
        out_shape=out_shape,
        scratch_shapes=[pltpu.VMEM((tm, d), BF16)],
        compiler_params=pltpu.CompilerParams(
            dimension_semantics=("arbitrary", "arbitrary"), vmem_limit_bytes=VMEM_LIMIT),
        name="norm_proj",
    )(x2d, g.reshape(1, d), w, col_scale.reshape(1, ntot))
    outs = list(outs)
    o1 = outs.pop(0) if n1 else None
    o2 = outs.pop(0) if n2 else None
    return o1, o2


def _out_ple_kernel(y_ref, x_ref, p_ref, wo_ref, wg_ref, wp_ref, fg_ref, o_ref, *, final):
    x2 = x_ref[...] + _dot(y_ref[...], wo_ref[...])
    gate = jax.nn.sigmoid(_dot(x2.astype(BF16), wg_ref[...]))
    proj = _dot(p_ref[...].astype(BF16), wp_ref[...])
    x3 = x2 + gate * proj
    if final:
        ms = jnp.mean(x3 * x3, axis=-1, keepdims=True)
        x3 = x3 * lax.rsqrt(ms + EPS) * fg_ref[...]
    o_ref[...] = x3


def _out_ple(y, x2d, p2d, wo, wg, wp, fg, *, final, tm=512):
    n, d = x2d.shape
    pd = p2d.shape[1]
    tm = min(tm, n)
    assert n % tm == 0
    const = lambda shape: pl.BlockSpec(shape, lambda i: (0, 0), pipeline_mode=pl.Buffered(1))
    return pl.pallas_call(
        functools.partial(_out_ple_kernel, final=final),
        grid=(n // tm,),
        in_specs=[
            pl.BlockSpec((tm, d), lambda i: (i, 0)),
            pl.BlockSpec((tm, d), lambda i: (i, 0)),
            pl.BlockSpec((tm, pd), lambda i: (i, 0)),
            const((d, d)), const((d, d)), const((pd, d)), const((1, d)),
        ],
        out_specs=pl.BlockSpec((tm, d), lambda i: (i, 0)),
        out_shape=jax.ShapeDtypeStruct((n, d), F32),
        compiler_params=pltpu.CompilerParams(
            dimension_semantics=("arbitrary",), vmem_limit_bytes=VMEM_LIMIT),
        name="out_ple",
    )(y, x2d, p2d, wo, wg, wp, fg.reshape(1, d))


def _sb_kernel(q_ref, k_ref, v_ref, z_ref, o_ref, carry_ref, acc_ref, *, tq, hp):
    i = pl.program_id(2)
    rows = lax.broadcasted_iota(jnp.int32, (tq, tq), 0)
    cols = lax.broadcasted_iota(jnp.int32, (tq, tq), 1)
    later = jnp.where(rows > cols, 1.0, 0.0).astype(BF16)
    strictly_causal = cols < rows

    def tile(kt, diag):
        start = pl.multiple_of(kt * tq, tq)
        heads = [slice(hh * LANE, (hh + 1) * LANE) for hh in range(hp)]
        zs = [_dot_nt(q_ref[0, :, cs], k_ref[0, pl.ds(start, tq), cs]) for cs in heads]
        l1ps = [jnp.log1p(jnp.exp(-jnp.abs(z))) for z in zs]
        lms, tails = [], []
        for z, l1p in zip(zs, l1ps):
            sp = jnp.maximum(z, 0.0) + l1p
            lm = jnp.where(strictly_causal, sp, 0.0) if diag else sp
            hi = lm.astype(BF16)
            lo = (lm - hi.astype(F32)).astype(BF16)
            lms.append(lm)
            tails.append(_dot(hi, later) + _dot(lo, later))
        for hh, (z, l1p, lm, tail) in enumerate(zip(zs, l1ps, lms, tails)):
            lsz = jnp.minimum(z, 0.0) - l1p
            carry = jnp.concatenate([carry_ref[hh]] * (tq // LANE), axis=1)
            a = jnp.exp(lsz - tail - carry)
            if diag:
                a = jnp.where(strictly_causal, a, 0.0)
            acc_ref[hh] += _dot(a.astype(BF16), v_ref[0, pl.ds(start, tq), heads[hh]])
            carry_ref[hh] += jnp.sum(lm, axis=1, keepdims=True)

    carry_ref[...] = jnp.zeros_like(carry_ref)
    acc_ref[...] = jnp.zeros_like(acc_ref)
    tile(i, True)

    def cond(st):
        kt, cmin = st
        return jnp.logical_and(kt >= 0, cmin < SB_UNDERFLOW)

    def body(st):
        kt, _ = st
        tile(kt, False)
        return kt - 1, jnp.min(carry_ref[...])

    lax.while_loop(cond, body, (i - 1, jnp.min(carry_ref[...])))
    for hh in range(hp):
        cs = slice(hh * LANE, (hh + 1) * LANE)
        o_ref[0, :, cs] = (acc_ref[hh] * _silu(z_ref[0, :, cs])).astype(o_ref.dtype)


def _sb_attention(qkv, zf, b, s, *, tq=256, hp=4):
    h = SB_HEADS
    tq = min(tq, s)
    assert s % tq == 0 and h % hp == 0
    hb = h // hp
    w = hp * LANE
    return pl.pallas_call(
        functools.partial(_sb_kernel, tq=tq, hp=hp),
        grid=(b, hb, s // tq),
        in_specs=[
            pl.BlockSpec((1, tq, w), lambda bi, hi, i: (bi, i, hi)),
            pl.BlockSpec((1, s, w), lambda bi, hi, i: (bi, 0, hb + hi)),
            pl.BlockSpec((1, s, w), lambda bi, hi, i: (bi, 0, 2 * hb + hi)),
            pl.BlockSpec((1, tq, w), lambda bi, hi, i: (bi, i, hi)),
        ],
        out_specs=pl.BlockSpec((1, tq, w), lambda bi, hi, i: (bi, i, hi)),
        out_shape=jax.ShapeDtypeStruct((b, s, h * LANE), BF16),
        scratch_shapes=[pltpu.VMEM((hp, tq, LANE), F32), pltpu.VMEM((hp, tq, LANE), F32)],
        compiler_params=pltpu.CompilerParams(
            dimension_semantics=("arbitrary", "arbitrary", "arbitrary"),
            vmem_limit_bytes=VMEM_LIMIT),
        name="sb_attention",
    )(qkv, qkv, qkv, zf)


def _cmp_kernel(x_ref, pos_ref, w1_ref, w2_ref, o_ref, *, ncb):
    half = NSA_CMP_LEN // 2
    y1 = jnp.zeros((ncb, LANE), F32)
    y2 = jnp.zeros((ncb, LANE), F32)
    for l in range(NSA_CMP_LEN):
        xl = x_ref[0, pl.ds(l % half, ncb, stride=NSA_CMP_STRIDE), :]
        xl = (xl + pos_ref[0, l:l + 1, :]).astype(BF16)
        part = _dot(xl, w1_ref[0, l * LANE:(l + 1) * LANE, :])
        if l < half:
            y1 = y1 + part
        else:
            y2 = y2 + part
    pre = y1 + pltpu.roll(y2, ncb - 1, 0)
    out = _dot(_gelu(pre).astype(BF16), w2_ref[0])
    valid = lax.broadcasted_iota(jnp.int32, (ncb, LANE), 0) < ncb - 1
    o_ref[0, 0, 0] = jnp.where(valid, out, 0.0).astype(o_ref.dtype)


def _nsa_compress(pf, pos, w1, w2, b, s, col0):
    g = NSA_GROUPS
    ncb = s // NSA_CMP_STRIDE
    return pl.pallas_call(
        functools.partial(_cmp_kernel, ncb=ncb),
        grid=(2, b, g),
        in_specs=[
            pl.BlockSpec((1, s, LANE), lambda kv, bi, gi: (bi, 0, col0 + kv * g + gi)),
            pl.BlockSpec((1, NSA_CMP_LEN, LANE), lambda kv, bi, gi: (kv, 0, 0)),
            pl.BlockSpec((1, NSA_CMP_LEN * LANE, LANE), lambda kv, bi, gi: (kv, 0, 0)),
            pl.BlockSpec((1, LANE, LANE), lambda kv, bi, gi: (kv, 0, 0)),
        ],
        out_specs=pl.BlockSpec((1, 1, 1, ncb, LANE), lambda kv, bi, gi: (kv, bi, gi, 0, 0)),
        out_shape=jax.ShapeDtypeStruct((2, b, g, ncb, LANE), BF16),
        compiler_params=pltpu.CompilerParams(
            dimension_semantics=("arbitrary", "arbitrary", "arbitrary"),
            vmem_limit_bytes=VMEM_LIMIT),
        name="nsa_compress",
    )(pf, pos, w1, w2)


def _split3(x):
    x = np.asarray(x, np.float32)
    bf = ml_dtypes.bfloat16
    hi = x.astype(bf).astype(np.float32)
    mid = (x - hi).astype(bf).astype(np.float32)
    lo = (x - hi - mid).astype(bf).astype(np.float32)
    return hi, mid, lo


def _nsa_constants(s, tq, tk):
    h, g, hg = NSA_HEADS, NSA_GROUPS, NSA_HG
    bf = ml_dtypes.bfloat16
    slopes = np.exp2(-8.0 * np.arange(1, h + 1) / h).astype(np.float32)
    qbase = np.zeros((g, hg * tq, LANE), np.float32)
    for gi in range(g):
        for n in range(hg):
            pieces = np.stack(_split3(slopes[gi * hg + n]))
            qbase[gi, n * tq:(n + 1) * tq, 0:3] = pieces
            qbase[gi, n * tq:(n + 1) * tq, 3:6] = pieces
    pos = np.arange(s)
    kaug = np.zeros((s, LANE), np.float32)
    kaug[:, 0:3] = (pos % tk)[:, None]
    kaug[:, 3:6] = (pos - pos % tk)[:, None]
    kaug[pos, NSA_IND_LANE + (pos // NSA_SEL_LEN) % NSA_IND_LANE] = 1.0
    ratio = NSA_SEL_LEN // NSA_CMP_STRIDE
    nsel = s // NSA_SEL_LEN
    cc = np.arange(s // NSA_CMP_STRIDE)
    cend = (ratio * (cc % nsel) + cc // nsel) * NSA_CMP_STRIDE + (NSA_CMP_LEN - 1)
    caug = np.zeros((cc.size, LANE), np.float32)
    caug[:, 0:3] = (cend % tk)[:, None]
    caug[:, 3:6] = (cend - cend % tk)[:, None]
    return (jnp.asarray(qbase.astype(bf)), jnp.asarray(kaug.astype(bf)), jnp.asarray(caug.astype(bf)))


def _nsa_kernel(q_ref, ks_ref, vs_ref, kw_ref, vw_ref, kaug_ref, kc_ref, vc_ref, caug_ref, qbase_ref,
                gate_ref, z_ref, y_ref, qp_ref, m_ref, l_ref, acc_ref, flag_ref, *, tq, tk, nsel):
    i = pl.program_id(2)
    hg = NSA_HG
    r = hg * tq
    ratio = NSA_SEL_LEN // NSA_CMP_STRIDE
    ncb = nsel * ratio
    bpt = tk // NSA_SEL_LEN
    n_tiles = nsel // bpt
    halves_per_tile = NSA_IND_LANE // bpt
    t0 = i * tq

    qg = q_ref[0]
    qall = jnp.concatenate([qg[:, n * LANE:(n + 1) * LANE] for n in range(hg)], axis=0)
    base = qbase_ref[0]
    qwin = jnp.concatenate([qall, base], axis=1)
    trow = t0 + lax.broadcasted_iota(jnp.int32, (r, 1), 0) % tq
    last = (t0 + tq - 1) // tk
    dist0 = (lax.broadcasted_iota(jnp.int32, (r, tk), 0) % tq
             - lax.broadcasted_iota(jnp.int32, (r, tk), 1))

    n_win = NSA_WINDOW // tk + 1
    win_sc, win_v = [], []
    for j in range(n_win):
        kt = last - (n_win - 1) + j
        start = pl.multiple_of(jnp.maximum(kt, 0) * tk, tk)
        kp = jnp.concatenate([kw_ref[0, pl.ds(start, tk), :], kaug_ref[pl.ds(start, tk), :]], axis=1)
        dist = dist0 - (kt * tk - t0)
        mask = (dist >= 0) & (dist < NSA_WINDOW) & (kt >= 0)
        sc = jnp.where(mask, _dot_nt(qwin, kp), NEG)
        win_sc += [sc[:, c * LANE:(c + 1) * LANE] for c in range(tk // LANE)]
        win_v.append(vw_ref[0, pl.ds(start, tk), :])
    m_w = win_sc[0]
    for ch in win_sc[1:]:
        m_w = jnp.maximum(m_w, ch)
    m_w = jnp.max(m_w, axis=1, keepdims=True)
    win_p = [jnp.exp(ch - m_w) for ch in win_sc]
    l_w = win_p[0]
    for x in win_p[1:]:
        l_w = l_w + x
    l_w = jnp.sum(l_w, axis=1, keepdims=True)
    cpt = tk // LANE
    o_w = _dot(jnp.concatenate([x.astype(BF16) for x in win_p[:cpt]], axis=1), win_v[0])
    for j in range(1, n_win):
        o_w = o_w + _dot(jnp.concatenate([x.astype(BF16) for x in win_p[j * cpt:(j + 1) * cpt]], axis=1),
                         win_v[j])
    o_w = o_w / l_w

    cc = lax.broadcasted_iota(jnp.int32, (1, ncb), 1)
    cend = (ratio * (cc % nsel) + cc // nsel) * NSA_CMP_STRIDE + (NSA_CMP_LEN - 1)
    mask_c = trow >= cend
    kcp = jnp.concatenate([kc_ref[0, 0], caug_ref[...]], axis=1)
    s_c = jnp.where(mask_c, _dot_nt(qwin, kcp), NEG)
    m_c = jnp.max(s_c, axis=1, keepdims=True)
    e_c = jnp.where(mask_c, jnp.exp(s_c - m_c), 0.0)
    l_c = jnp.sum(e_c, axis=1, keepdims=True)
    p_c = e_c * jnp.where(l_c > 0.0, 1.0 / l_c, 0.0)
    o_c = _dot(p_c.astype(BF16), vc_ref[0, 0])

    ph = p_c[0:tq]
    for n in range(1, hg):
        ph = ph + p_c[n * tq:(n + 1) * tq]
    imp = ph[:, 0:nsel]
    for a in range(1, ratio):
        imp = imp + ph[:, a * nsel:(a + 1) * nsel]
    if nsel < LANE:
        imp = jnp.concatenate([imp, jnp.zeros((tq, LANE - nsel), F32)], axis=1)

    imp_t = imp.T
    jj = lax.broadcasted_iota(jnp.int32, (LANE, tq), 0)
    cur = (t0 + lax.broadcasted_iota(jnp.int32, (1, tq), 1)) // NSA_SEL_LEN
    forced = (jj == 0) | (jj == cur) | (jj == cur - 1)
    val = jnp.where(forced, BIG, jnp.where(jj <= cur, imp_t, -BIG))
    sel_t = jnp.zeros((LANE, tq), F32)
    for _ in range(min(NSA_SEL_TOPK, nsel)):
        vmax = jnp.max(val, axis=0, keepdims=True)
        first = jnp.min(jnp.where(val == vmax, jj, LANE), axis=0, keepdims=True)
        hit = jj == first
        sel_t = jnp.where(hit, 1.0, sel_t)
        val = jnp.where(hit, -3e38, val)
    sel = sel_t.T

    col_any = jnp.max(sel, axis=0, keepdims=True)
    for kt in range(n_tiles):
        flag_ref[kt] = (jnp.max(col_any[:, kt * bpt:(kt + 1) * bpt]) > 0.5).astype(jnp.int32)

    lane_q = lax.broadcasted_iota(jnp.int32, (tq, LANE), 1)
    lane_r = lax.broadcasted_iota(jnp.int32, (r, LANE), 1)
    unpicked = jnp.where(sel > 0.5, 0.0, -NSA_MASK)
    for half, part in enumerate((pltpu.roll(unpicked, NSA_IND_LANE, 1), unpicked)):
        part = jnp.where(lane_q >= NSA_IND_LANE, part, 0.0).astype(BF16)
        aug = jnp.where(lane_r >= NSA_IND_LANE, jnp.concatenate([part] * hg, axis=0), base)
        qp_ref[half] = jnp.concatenate([qall, aug], axis=1)

    m_ref[...] = jnp.full_like(m_ref, NEG)
    l_ref[...] = jnp.zeros_like(l_ref)
    acc_ref[...] = jnp.zeros_like(acc_ref)

    def flash_tile(kt, causal):
        start = pl.multiple_of(kt * tk, tk)
        kp = jnp.concatenate([ks_ref[0, pl.ds(start, tk), :], kaug_ref[pl.ds(start, tk), :]], axis=1)
        sc = _dot_nt(qp_ref[kt // halves_per_tile], kp)
        if causal:
            sc = jnp.where(dist0 >= start - t0, sc, NEG)
        chunks = [sc[:, c * LANE:(c + 1) * LANE] for c in range(tk // LANE)]
        cmax = chunks[0]
        for ch in chunks[1:]:
            cmax = jnp.maximum(cmax, ch)
        m_old = m_ref[...]
        m_new = jnp.maximum(m_old, jnp.max(cmax, axis=1, keepdims=True))
        alpha = jnp.exp(m_old - m_new)
        ps = [jnp.exp(ch - m_new) for ch in chunks]
        psum = ps[0]
        for x in ps[1:]:
            psum = psum + x
        l_ref[...] = alpha * l_ref[...] + jnp.sum(psum, axis=1, keepdims=True)
        p = jnp.concatenate([x.astype(BF16) for x in ps], axis=1)
        acc_ref[...] = alpha * acc_ref[...] + _dot(p, vs_ref[0, pl.ds(start, tk), :])
        m_ref[...] = m_new

    def sel_body(kt, c):
        @pl.when(flag_ref[kt] > 0)
        def _():
            flash_tile(kt, False)
        return c

    lax.fori_loop(0, last, sel_body, 0)
    flash_tile(last, True)
    o_s = acc_ref[...] / l_ref[...]

    gates = jax.nn.sigmoid(gate_ref[0])
    zg = z_ref[0]
    for n in range(hg):
        rs = slice(n * tq, (n + 1) * tq)
        o = (gates[:, n:n + 1] * o_c[rs]
             + gates[:, hg + n:hg + n + 1] * o_s[rs]
             + gates[:, 2 * hg + n:2 * hg + n + 1] * o_w[rs])
        cs = slice(n * LANE, (n + 1) * LANE)
        y_ref[0, :, cs] = (o * _silu(zg[:, cs])).astype(y_ref.dtype)


def _nsa_attention(pb, pf, kvc, b, s, *, tq=128, tk=256):
    g, hg = NSA_GROUPS, NSA_HG
    tq = min(tq, s)
    tk = min(tk, s)
    nsel = s // NSA_SEL_LEN
    ncb = s // NSA_CMP_STRIDE
    r = hg * tq
    assert s % tk == 0 and tk % tq == 0 and tk % LANE == 0 and nsel <= LANE
    assert NSA_IND_LANE % (tk // NSA_SEL_LEN) == 0 and tk <= 256
    qbase, kaug, caug = _nsa_constants(s, tq, tk)
    qblocks = NSA_HEADS
    zblocks = NSA_HEADS
    kv_spec = lambda off: pl.BlockSpec((1, s, LANE), lambda bi, gi, i: (bi, 0, qblocks + off * g + gi))
    return pl.pallas_call(
        functools.partial(_nsa_kernel, tq=tq, tk=tk, nsel=nsel),
        grid=(b, g, s // tq),
        in_specs=[
            pl.BlockSpec((1, tq, hg * LANE), lambda bi, gi, i: (bi, i, gi)),
            kv_spec(0), kv_spec(1), kv_spec(2), kv_spec(3),
            pl.BlockSpec((s, LANE), lambda bi, gi, i: (0, 0)),
            pl.BlockSpec((1, 1, ncb, LANE), lambda bi, gi, i: (bi, gi, 0, 0)),
            pl.BlockSpec((1, 1, ncb, LANE), lambda bi, gi, i: (bi, gi, 0, 0)),
            pl.BlockSpec((ncb, LANE), lambda bi, gi, i: (0, 0)),
            pl.BlockSpec((1, r, LANE), lambda bi, gi, i: (gi, 0, 0)),
            pl.BlockSpec((1, tq, LANE), lambda bi, gi, i: (bi, i, zblocks + gi)),
            pl.BlockSpec((1, tq, hg * LANE), lambda bi, gi, i: (bi, i, gi)),
        ],
        out_specs=pl.BlockSpec((1, tq, hg * LANE), lambda bi, gi, i: (bi, i, gi)),
        out_shape=jax.ShapeDtypeStruct((b, s, NSA_HEADS * LANE), BF16),
        scratch_shapes=[
            pltpu.VMEM((2, r, 2 * LANE), BF16),
            pltpu.VMEM((r, LANE), F32), pltpu.VMEM((r, LANE), F32), pltpu.VMEM((r, LANE), F32),
            pltpu.SMEM((nsel // (tk // NSA_SEL_LEN),), jnp.int32),
        ],
        compiler_params=pltpu.CompilerParams(
            dimension_semantics=("arbitrary", "arbitrary", "arbitrary"),
            vmem_limit_bytes=VMEM_LIMIT),
        name="nsa_attention",
    )(pb, pb, pb, pb, pb, kaug, kvc[0], kvc[1], caug, qbase, pf, pf)


def _gmlp_kernel(u_ref, v_ref, z_ref, vg_ref, ws_ref, bst_ref, y_ref, *, n_chunks):
    c = GM_CHUNK
    v = _gelu(v_ref[...])
    ms = jnp.mean(v * v, axis=-1, keepdims=True)
    vn = (v * lax.rsqrt(ms + EPS) * vg_ref[...]).astype(BF16)
    rows = lax.broadcasted_iota(jnp.int32, (c, c), 0)
    cols = lax.broadcasted_iota(jnp.int32, (c, c), 1)
    causal = cols <= rows
    bst = bst_ref[...]
    for gi in range(GM_GROUPS):
        ws = jnp.where(causal, ws_ref[gi], 0.0).astype(BF16)
        cs = slice(gi * LANE, (gi + 1) * LANE)
        for ci in range(n_chunks):
            rs = slice(ci * c, (ci + 1) * c)
            mixed = _dot(ws, vn[rs, cs]) + bst[:, gi:gi + 1]
            y = _gelu(u_ref[rs, cs]) * mixed * _silu(z_ref[rs, cs])
            y_ref[rs, cs] = y.astype(y_ref.dtype)


def _gmlp(pf, vg, ws, bs, *, n_chunks=2):
    n = pf.shape[0]
    w = pf.shape[1] // 3
    tm = GM_CHUNK * n_chunks
    assert n % tm == 0 and w == GM_GROUPS * LANE
    return pl.pallas_call(
        functools.partial(_gmlp_kernel, n_chunks=n_chunks),
        grid=(n // tm,),
        in_specs=[
            pl.BlockSpec((tm, w), lambda i: (i, 0)),
            pl.BlockSpec((tm, w), lambda i: (i, 1)),
            pl.BlockSpec((tm, w), lambda i: (i, 2)),
            pl.BlockSpec((1, w), lambda i: (0, 0)),
            pl.BlockSpec((GM_GROUPS, GM_CHUNK, GM_CHUNK), lambda i: (0, 0, 0)),
            pl.BlockSpec((GM_CHUNK, GM_GROUPS), lambda i: (0, 0)),
        ],
        out_specs=pl.BlockSpec((tm, w), lambda i: (i, 0)),
        out_shape=jax.ShapeDtypeStruct((n, w), BF16),
        compiler_params=pltpu.CompilerParams(
            dimension_semantics=("arbitrary",), vmem_limit_bytes=VMEM_LIMIT),
        name="gmlp_mix",
    )(pf, pf, pf, vg.reshape(1, w), ws, bs.T)


def _sb_layer(x2d, b, s, norm_g, w_in, w_out):
    width = w_in.shape[1] // 4
    scale = np.float32(LANE ** -0.5)
    col_scale = jnp.concatenate([jnp.full((width,), scale, F32), jnp.ones((3 * width,), F32)])
    qkv, zf = _norm_proj(x2d, norm_g, w_in.astype(BF16), col_scale, 3 * width)
    y = _sb_attention(qkv.reshape(b, s, 3 * width), zf.reshape(b, s, width), b, s)
    return y.reshape(b * s, width), w_out.astype(BF16)


def _nsa_layer(x2d, b, s, norm_g, w_in, pos_k, pos_v, ck_w1, ck_w2, cv_w1, cv_w2, w_out):
    d = x2d.shape[1]
    h, g, hg = NSA_HEADS, NSA_GROUPS, NSA_HG
    kvw = g * LANE
    qw = h * LANE
    o_kc, o_vc, o_ks, o_vs, o_kw, o_vw = (qw + a * kvw for a in range(6))
    o_g = qw + 6 * kvw
    o_z = o_g + NSA_BRANCHES * h
    wg = w_in[:, o_g:o_z].reshape(d, NSA_BRANCHES, g, hg).transpose(0, 2, 1, 3)
    wg = wg.reshape(d, g, NSA_BRANCHES * hg)
    wg = jnp.pad(wg, ((0, 0), (0, 0), (0, LANE - NSA_BRANCHES * hg))).reshape(d, g * LANE)
    w_all = jnp.concatenate([
        w_in[:, :qw], w_in[:, o_ks:o_g],
        w_in[:, o_z:], wg, w_in[:, o_kc:o_ks],
    ], axis=1).astype(BF16)
    n1 = qw + 4 * kvw
    scale = np.float32(LANE ** -0.5)
    col_scale = jnp.concatenate([jnp.full((qw,), scale, F32), jnp.ones((w_all.shape[1] - qw,), F32)])
    pb, pf = _norm_proj(x2d, norm_g, w_all, col_scale, n1)
    pf = pf.reshape(b, s, pf.shape[1])
    pb = pb.reshape(b, s, n1)

    pos = jnp.stack([pos_k, pos_v])
    w1 = jnp.stack([ck_w1, cv_w1]).astype(BF16)
    w2 = jnp.stack([ck_w2, cv_w2]).astype(BF16)
    kvc = _nsa_compress(pf, pos, w1, w2, b, s, (qw + g * LANE) // LANE)
    ratio = NSA_SEL_LEN // NSA_CMP_STRIDE
    nsel = s // NSA_SEL_LEN
    kvc = kvc.reshape(2, b, g, nsel, ratio, LANE).transpose(0, 1, 2, 4, 3, 5).reshape(2, b, g, nsel * ratio, LANE)

    y = _nsa_attention(pb, pf, kvc, b, s)
    return y.reshape(b * s, qw), w_out.astype(BF16)


def _gmlp_layer(x2d, norm_g, w_in, v_norm_g, w_s, b_s, w_out):
    ntot = w_in.shape[1]
    _, pf = _norm_proj(x2d, norm_g, w_in.astype(BF16), jnp.ones((ntot,), F32), 0)
    y = _gmlp(pf, v_norm_g, w_s, b_s)
    return y, w_out.astype(BF16)


def kernel(x, p, norm_g, final_norm_g, ple_proj, ple_gate, sb_w_in, sb_w_out, nsa_w_in, nsa_cmp_pos_k, nsa_cmp_pos_v, nsa_cmp_k_w1, nsa_cmp_k_w2, nsa_cmp_v_w1, nsa_cmp_v_w2, nsa_w_out, gm_w_in, gm_v_norm_g, gm_w_s, gm_b_s, gm_w_out):
    b, s, d = x.shape
    depth = p.shape[0]
    x2d = x.reshape(b * s, d)
    for i in range(depth):
        kind, j = i % 3, i // 3
        if kind == 0:
            y, wo = _sb_layer(x2d, b, s, norm_g[i], sb_w_in[j], sb_w_out[j])
        elif kind == 1:
            y, wo = _nsa_layer(x2d, b, s, norm_g[i], nsa_w_in[j], nsa_cmp_pos_k[j], nsa_cmp_pos_v[j],
                               nsa_cmp_k_w1[j], nsa_cmp_k_w2[j], nsa_cmp_v_w1[j], nsa_cmp_v_w2[j],
                               nsa_w_out[j])
        else:
            y, wo = _gmlp_layer(x2d, norm_g[i], gm_w_in[j], gm_v_norm_g[j], gm_w_s[j], gm_b_s[j],
                                gm_w_out[j])
        x2d = _out_ple(y, x2d, p[i].reshape(b * s, p.shape[-1]), wo,
                       ple_gate[i].astype(BF16), ple_proj[i].astype(BF16), final_norm_g,
                       final=(i == depth - 1))
    return x2d.reshape(b, s, d)
```

```python
import functools

import ml_dtypes
import numpy as np
import jax
import jax.numpy as jnp
from jax import lax
from jax.experimental import pallas as pl
from jax.experimental.pallas import tpu as pltpu

F32 = jnp.float32
BF16 = jnp.bfloat16

EPS = 1e-6
NEG = -1e30
BIG = 1e4
LANE = 128
VMEM_LIMIT = 56 * 1024 * 1024

PLE_DIM = 256
SB_HEADS = 16
NSA_HEADS = 16
NSA_GROUPS = 4
NSA_HG = NSA_HEADS // NSA_GROUPS
NSA_CMP_LEN = 32
NSA_CMP_STRIDE = 16
NSA_SEL_LEN = 64
NSA_SEL_TOPK = 16
NSA_WINDOW = 512
NSA_BRANCHES = 3
GM_GROUPS = 16
GM_CHUNK = 128

SB_UNDERFLOW = 110.0
NSA_MASK = 2.0 ** 100
NSA_IND_LANE = LANE // 2


def _dot(a, b):
    return jnp.dot(a, b, preferred_element_type=F32)


def _dot_nt(a, b):
    return lax.dot_general(a, b, (((1,), (1,)), ((), ())), preferred_element_type=F32)


def _gelu(x):
    return 0.5 * x * (1.0 + lax.erf(x * np.float32(1.0 / np.sqrt(2.0))))


def _silu(x):
    return x * jax.nn.sigmoid(x)


def _norm_proj_kernel(x_ref, g_ref, w_ref, cs_ref, *rest, nb1, n_out):
    hn_ref = rest[-1]
    outs = rest[:n_out]
    j = pl.program_id(1)

    @pl.when(j == 0)
    def _():
        xf = x_ref[...]
        ms = jnp.mean(xf * xf, axis=-1, keepdims=True)
        hn_ref[...] = (xf * lax.rsqrt(ms + EPS) * g_ref[...]).astype(BF16)

    def compute():
        return _dot(hn_ref[...], w_ref[...]) * cs_ref[...]

    if n_out == 1:
        outs[0][...] = compute().astype(outs[0].dtype)
    else:
        @pl.when(j < nb1)
        def _():
            outs[0][...] = compute().astype(outs[0].dtype)

        @pl.when(j >= nb1)
        def _():
            outs[1][...] = compute().astype(outs[1].dtype)


def _norm_proj(x2d, g, w, col_scale, n1, *, tm=1024, tn=512):
    n, d = x2d.shape
    ntot = w.shape[1]
    n2 = ntot - n1
    tm = min(tm, n)
    assert n % tm == 0 and n1 % tn == 0 and n2 % tn == 0
    nb1 = n1 // tn
    out_shape, out_specs = [], []
    if n1:
        out_shape.append(jax.ShapeDtypeStruct((n, n1), BF16))
        out_specs.append(pl.BlockSpec((tm, tn), lambda i, j: (i, jnp.minimum(j, nb1 - 1))))
    if n2:
        out_shape.append(jax.ShapeDtypeStruct((n, n2), F32))
        out_specs.append(pl.BlockSpec((tm, tn), lambda i, j: (i, jnp.maximum(j - nb1, 0))))
    outs = pl.pallas_call(
        functools.partial(_norm_proj_kernel, nb1=nb1, n_out=len(out_shape)),
        grid=(n // tm, ntot // tn),
        in_specs=[
            pl.BlockSpec((tm, d), lambda i, j: (i, 0)),
            pl.BlockSpec((1, d), lambda i, j: (0, 0)),
            pl.BlockSpec((d, tn), lambda i, j: (0, j)),
            pl.BlockSpec((1, tn), lambda i, j: (0, j)),
        ],
        out_specs=out_specs,
        out_shape=out_shape,
        scratch_shapes=[pltpu.VMEM((tm, d), BF16)],
        compiler_params=pltpu.CompilerParams(
            dimension_semantics=("arbitrary", "arbitrary"), vmem_limit_bytes=VMEM_LIMIT),
        name="norm_proj",
    )(x2d, g.reshape(1, d), w, col_scale.reshape(1, ntot))
    outs = list(outs)
    o1 = outs.pop(0) if n1 else None
    o2 = outs.pop(0) if n2 else None
    return o1, o2


def _out_ple_kernel(y_ref, x_ref, p_ref, wo_ref, wg_ref, wp_ref, fg_ref, o_ref, *, final):
    x2 = x_ref[...] + _dot(y_ref[...], wo_ref[...])
    gate = jax.nn.sigmoid(_dot(x2.astype(BF16), wg_ref[...]))
    proj = _dot(p_ref[...].astype(BF16), wp_ref[...])
    x3 = x2 + gate * proj
    if final:
        ms = jnp.mean(x3 * x3, axis=-1, keepdims=True)
        x3 = x3 * lax.rsqrt(ms + EPS) * fg_ref[...]
    o_ref[...] = x3


def _out_ple(y, x2d, p2d, wo, wg, wp, fg, *, final, tm=512):
    n, d = x2d.shape
    pd = p2d.shape[1]
    tm = min(tm, n)
    assert n % tm == 0
    const = lambda shape: pl.BlockSpec(shape, lambda i: (0, 0), pipeline_mode=pl.Buffered(1))
    return pl.pallas_call(
        functools.partial(_out_ple_kernel, final=final),
        grid=(n // tm,),
        in_specs=[
            pl.BlockSpec((tm, d), lambda i: (i, 0)),
            pl.BlockSpec((tm, d), lambda i: (i, 0)),
            pl.BlockSpec((tm, pd), lambda i: (i, 0)),
            const((d, d)), const((d, d)), const((pd, d)), const((1, d)),
        ],
        out_specs=pl.BlockSpec((tm, d), lambda i: (i, 0)),
        out_shape=jax.ShapeDtypeStruct((n, d), F32),
        compiler_params=pltpu.CompilerParams(
            dimension_semantics=("arbitrary",), vmem_limit_bytes=VMEM_LIMIT),
        name="out_ple",
    )(y, x2d, p2d, wo, wg, wp, fg.reshape(1, d))


def _sb_kernel(q_ref, k_ref, v_ref, z_ref, o_ref, carry_ref, acc_ref, *, tq, hp):
    i = pl.program_id(2)
    rows = lax.broadcasted_iota(jnp.int32, (tq, tq), 0)
    cols = lax.broadcasted_iota(jnp.int32, (tq, tq), 1)
    later = jnp.where(rows > cols, 1.0, 0.0).astype(BF16)
    strictly_causal = cols < rows

    def tile(kt, diag):
        start = pl.multiple_of(kt * tq, tq)
        heads = [slice(hh * LANE, (hh + 1) * LANE) for hh in range(hp)]
        zs = [_dot_nt(q_ref[0, :, cs], k_ref[0, pl.ds(start, tq), cs]) for cs in heads]
        l1ps = [jnp.log1p(jnp.exp(-jnp.abs(z))) for z in zs]
        lms, tails = [], []
        for z, l1p in zip(zs, l1ps):
            sp = jnp.maximum(z, 0.0) + l1p
            lm = jnp.where(strictly_causal, sp, 0.0) if diag else sp
            hi = lm.astype(BF16)
            lo = (lm - hi.astype(F32)).astype(BF16)
            lms.append(lm)
            tails.append(_dot(hi, later) + _dot(lo, later))
        for hh, (z, l1p, lm, tail) in enumerate(zip(zs, l1ps, lms, tails)):
            lsz = jnp.minimum(z, 0.0) - l1p
            carry = jnp.concatenate([carry_ref[hh]] * (tq // LANE), axis=1)
            a = jnp.exp(lsz - tail - carry)
            if diag:
                a = jnp.where(strictly_causal, a, 0.0)
            acc_ref[hh] += _dot(a.astype(BF16), v_ref[0, pl.ds(start, tq), heads[hh]])
            carry_ref[hh] += jnp.sum(lm, axis=1, keepdims=True)

    carry_ref[...] = jnp.zeros_like(carry_ref)
    acc_ref[...] = jnp.zeros_like(acc_ref)
    tile(i, True)

    def cond(st):
        kt, cmin = st
        return jnp.logical_and(kt >= 0, cmin < SB_UNDERFLOW)

    def body(st):
        kt, _ = st
        tile(kt, False)
        return kt - 1, jnp.min(carry_ref[...])

    lax.while_loop(cond, body, (i - 1, jnp.min(carry_ref[...])))
    for hh in range(hp):
        cs = slice(hh * LANE, (hh + 1) * LANE)
        o_ref[0, :, cs] = (acc_ref[hh] * _silu(z_ref[0, :, cs])).astype(o_ref.dtype)


def _sb_attention(qkv, zf, b, s, *, tq=256, hp=4):
    h = SB_HEADS
    tq = min(tq, s)
    assert s % tq == 0 and h % hp == 0
    hb = h // hp
    w = hp * LANE
    return pl.pallas_call(
        functools.partial(_sb_kernel, tq=tq, hp=hp),
        grid=(b, hb, s // tq),
        in_specs=[
            pl.BlockSpec((1, tq, w), lambda bi, hi, i: (bi, i, hi)),
            pl.BlockSpec((1, s, w), lambda bi, hi, i: (bi, 0, hb + hi)),
            pl.BlockSpec((1, s, w), lambda bi, hi, i: (bi, 0, 2 * hb + hi)),
            pl.BlockSpec((1, tq, w), lambda bi, hi, i: (bi, i, hi)),
        ],
        out_specs=pl.BlockSpec((1, tq, w), lambda bi, hi, i: (bi, i, hi)),
        out_shape=jax.ShapeDtypeStruct((b, s, h * LANE), BF16),
        scratch_shapes=[pltpu.VMEM((hp, tq, LANE), F32), pltpu.VMEM((hp, tq, LANE), F32)],
        compiler_params=pltpu.CompilerParams(
            dimension_semantics=("arbitrary", "arbitrary", "arbitrary"),
            vmem_limit_bytes=VMEM_LIMIT),
        name="sb_attention",
    )(qkv, qkv, qkv, zf)


def _cmp_kernel(x_ref, pos_ref, w1_ref, w2_ref, o_ref, *, ncb):
    half = NSA_CMP_LEN // 2
    y1 = jnp.zeros((ncb, LANE), F32)
    y2 = jnp.zeros((ncb, LANE), F32)
    for l in range(NSA_CMP_LEN):
        xl = x_ref[0, pl.ds(l % half, ncb, stride=NSA_CMP_STRIDE), :]
        xl = (xl + pos_ref[0, l:l + 1, :]).astype(BF16)
        part = _dot(xl, w1_ref[0, l * LANE:(l + 1) * LANE, :])
        if l < half:
            y1 = y1 + part
        else:
            y2 = y2 + part
    pre = y1 + pltpu.roll(y2, ncb - 1, 0)
    out = _dot(_gelu(pre).astype(BF16), w2_ref[0])
    valid = lax.broadcasted_iota(jnp.int32, (ncb, LANE), 0) < ncb - 1
    o_ref[0, 0, 0] = jnp.where(valid, out, 0.0).astype(o_ref.dtype)


def _nsa_compress(pf, pos, w1, w2, b, s, col0):
    g = NSA_GROUPS
    ncb = s // NSA_CMP_STRIDE
    return pl.pallas_call(
        functools.partial(_cmp_kernel, ncb=ncb),
        grid=(2, b, g),
        in_specs=[
            pl.BlockSpec((1, s, LANE), lambda kv, bi, gi: (bi, 0, col0 + kv * g + gi)),
            pl.BlockSpec((1, NSA_CMP_LEN, LANE), lambda kv, bi, gi: (kv, 0, 0)),
            pl.BlockSpec((1, NSA_CMP_LEN * LANE, LANE), lambda kv, bi, gi: (kv, 0, 0)),
            pl.BlockSpec((1, LANE, LANE), lambda kv, bi, gi: (kv, 0, 0)),
        ],
        out_specs=pl.BlockSpec((1, 1, 1, ncb, LANE), lambda kv, bi, gi: (kv, bi, gi, 0, 0)),
        out_shape=jax.ShapeDtypeStruct((2, b, g, ncb, LANE), BF16),
        compiler_params=pltpu.CompilerParams(
            dimension_semantics=("arbitrary", "arbitrary", "arbitrary"),
            vmem_limit_bytes=VMEM_LIMIT),
        name="nsa_compress",
    )(pf, pos, w1, w2)


def _split3(x):
    x = np.asarray(x, np.float32)
    bf = ml_dtypes.bfloat16
    hi = x.astype(bf).astype(np.float32)
    mid = (x - hi).astype(bf).astype(np.float32)
    lo = (x - hi - mid).astype(bf).astype(np.float32)
    return hi, mid, lo


def _nsa_constants(s, tq, tk):
    h, g, hg = NSA_HEADS, NSA_GROUPS, NSA_HG
    bf = ml_dtypes.bfloat16
    slopes = np.exp2(-8.0 * np.arange(1, h + 1) / h).astype(np.float32)
    qbase = np.zeros((g, hg * tq, LANE), np.float32)
    for gi in range(g):
        for n in range(hg):
            pieces = np.stack(_split3(slopes[gi * hg + n]))
            qbase[gi, n * tq:(n + 1) * tq, 0:3] = pieces
            qbase[gi, n * tq:(n + 1) * tq, 3:6] = pieces
    pos = np.arange(s)
    kaug = np.zeros((s, LANE), np.float32)
    kaug[:, 0:3] = (pos % tk)[:, None]
    kaug[:, 3:6] = (pos - pos % tk)[:, None]
    kaug[pos, NSA_IND_LANE + (pos // NSA_SEL_LEN) % NSA_IND_LANE] = 1.0
    ratio = NSA_SEL_LEN // NSA_CMP_STRIDE
    nsel = s // NSA_SEL_LEN
    cc = np.arange(s // NSA_CMP_STRIDE)
    cend = (ratio * (cc % nsel) + cc // nsel) * NSA_CMP_STRIDE + (NSA_CMP_LEN - 1)
    caug = np.zeros((cc.size, LANE), np.float32)
    caug[:, 0:3] = (cend % tk)[:, None]
    caug[:, 3:6] = (cend - cend % tk)[:, None]
    return (jnp.asarray(qbase.astype(bf)), jnp.asarray(kaug.astype(bf)), jnp.asarray(caug.astype(bf)))


def _nsa_kernel(q_ref, ks_ref, vs_ref, kw_ref, vw_ref, kaug_ref, kc_ref, vc_ref, caug_ref, qbase_ref,
                gate_ref, z_ref, y_ref, qp_ref, m_ref, l_ref, acc_ref, sc_ref, flag_ref, tiles_ref,
                *, tq, tk, nsel):
    i = pl.program_id(2)
    hg = NSA_HG
    r = hg * tq
    ratio = NSA_SEL_LEN // NSA_CMP_STRIDE
    ncb = nsel * ratio
    bpt = tk // NSA_SEL_LEN
    n_tiles = nsel // bpt
    halves_per_tile = NSA_IND_LANE // bpt
    t0 = i * tq

    qg = q_ref[0]
    qall = jnp.concatenate([qg[:, n * LANE:(n + 1) * LANE] for n in range(hg)], axis=0)
    base = qbase_ref[0]
    qwin = jnp.concatenate([qall, base], axis=1)
    trow = t0 + lax.broadcasted_iota(jnp.int32, (r, 1), 0) % tq
    last = (t0 + tq - 1) // tk
    dist0 = (lax.broadcasted_iota(jnp.int32, (r, tk), 0) % tq
             - lax.broadcasted_iota(jnp.int32, (r, tk), 1))

    n_win = NSA_WINDOW // tk + 1
    win_sc, win_v = [], []
    for j in range(n_win):
        kt = last - (n_win - 1) + j
        start = pl.multiple_of(jnp.maximum(kt, 0) * tk, tk)
        kp = jnp.concatenate([kw_ref[0, pl.ds(start, tk), :], kaug_ref[pl.ds(start, tk), :]], axis=1)
        off = kt * tk - t0
        if j == n_win - 1:
            mask = dist0 >= off
        elif j == 0:
            mask = dist0 < jnp.where(kt >= 0, off + NSA_WINDOW, -tk)
        else:
            mask = kt >= 0
        sc = jnp.where(mask, _dot_nt(qwin, kp), NEG)
        win_sc += [sc[:, c * LANE:(c + 1) * LANE] for c in range(tk // LANE)]
        win_v.append(vw_ref[0, pl.ds(start, tk), :])
    m_w = win_sc[0]
    for ch in win_sc[1:]:
        m_w = jnp.maximum(m_w, ch)
    m_w = jnp.max(m_w, axis=1, keepdims=True)
    win_p = [jnp.exp(ch - m_w) for ch in win_sc]
    l_w = win_p[0]
    for x in win_p[1:]:
        l_w = l_w + x
    l_w = jnp.sum(l_w, axis=1, keepdims=True)
    cpt = tk // LANE
    o_w = _dot(jnp.concatenate([x.astype(BF16) for x in win_p[:cpt]], axis=1), win_v[0])
    for j in range(1, n_win):
        o_w = o_w + _dot(jnp.concatenate([x.astype(BF16) for x in win_p[j * cpt:(j + 1) * cpt]], axis=1),
                         win_v[j])
    o_w = o_w / l_w

    cc = lax.broadcasted_iota(jnp.int32, (1, ncb), 1)
    cend = (ratio * (cc % nsel) + cc // nsel) * NSA_CMP_STRIDE + (NSA_CMP_LEN - 1)
    mask_c = trow >= cend
    kcp = jnp.concatenate([kc_ref[0, 0], caug_ref[...]], axis=1)
    s_c = jnp.where(mask_c, _dot_nt(qwin, kcp), NEG)
    m_c = jnp.max(s_c, axis=1, keepdims=True)
    e_c = jnp.where(mask_c, jnp.exp(s_c - m_c), 0.0)
    l_c = jnp.sum(e_c, axis=1, keepdims=True)
    p_c = e_c * jnp.where(l_c > 0.0, 1.0 / l_c, 0.0)
    o_c = _dot(p_c.astype(BF16), vc_ref[0, 0])

    ph = p_c[0:tq]
    for n in range(1, hg):
        ph = ph + p_c[n * tq:(n + 1) * tq]
    imp = ph[:, 0:nsel]
    for a in range(1, ratio):
        imp = imp + ph[:, a * nsel:(a + 1) * nsel]
    if nsel < LANE:
        imp = jnp.concatenate([imp, jnp.zeros((tq, LANE - nsel), F32)], axis=1)

    imp_t = imp.T
    jj = lax.broadcasted_iota(jnp.int32, (LANE, tq), 0)
    cur = (t0 + lax.broadcasted_iota(jnp.int32, (1, tq), 1)) // NSA_SEL_LEN
    forced = (jj == 0) | (jj == cur) | (jj == cur - 1)
    val = jnp.where(forced, BIG, jnp.where(jj <= cur, imp_t, -BIG))
    sel_t = jnp.zeros((LANE, tq), F32)
    for _ in range(min(NSA_SEL_TOPK, nsel)):
        vmax = jnp.max(val, axis=0, keepdims=True)
        first = jnp.min(jnp.where(val == vmax, jj, LANE), axis=0, keepdims=True)
        hit = jj == first
        sel_t = jnp.where(hit, 1.0, sel_t)
        val = jnp.where(hit, -3e38, val)
    sel = sel_t.T

    col_any = jnp.max(sel, axis=0, keepdims=True)
    for kt in range(n_tiles):
        flag_ref[kt] = (jnp.max(col_any[:, kt * bpt:(kt + 1) * bpt]) > 0.5).astype(jnp.int32)

    lane_q = lax.broadcasted_iota(jnp.int32, (tq, LANE), 1)
    lane_r = lax.broadcasted_iota(jnp.int32, (r, LANE), 1)
    unpicked = jnp.where(sel > 0.5, 0.0, -NSA_MASK)
    for half, part in enumerate((pltpu.roll(unpicked, NSA_IND_LANE, 1), unpicked)):
        part = jnp.where(lane_q >= NSA_IND_LANE, part, 0.0).astype(BF16)
        aug = jnp.where(lane_r >= NSA_IND_LANE, jnp.concatenate([part] * hg, axis=0), base)
        qp_ref[half] = jnp.concatenate([qall, aug], axis=1)

    m_ref[...] = jnp.full_like(m_ref, NEG)
    l_ref[...] = jnp.zeros_like(l_ref)
    acc_ref[...] = jnp.zeros_like(acc_ref)

    def compact(kt, n):
        tiles_ref[n] = kt
        return n + flag_ref[kt]

    n_act = lax.fori_loop(0, last, compact, 0)
    tiles_ref[n_act] = last

    def scores(kt):
        start = pl.multiple_of(kt * tk, tk)
        kp = jnp.concatenate([ks_ref[0, pl.ds(start, tk), :], kaug_ref[pl.ds(start, tk), :]], axis=1)
        return _dot_nt(qp_ref[kt // halves_per_tile], kp)

    def softmax_step(sc, kt):
        start = pl.multiple_of(kt * tk, tk)
        chunks = [sc[:, c * LANE:(c + 1) * LANE] for c in range(tk // LANE)]
        cmax = chunks[0]
        for ch in chunks[1:]:
            cmax = jnp.maximum(cmax, ch)
        m_old = m_ref[...]
        m_new = jnp.maximum(m_old, jnp.max(cmax, axis=1, keepdims=True))
        alpha = jnp.exp(m_old - m_new)
        ps = [jnp.exp(ch - m_new) for ch in chunks]
        psum = ps[0]
        for x in ps[1:]:
            psum = psum + x
        l_ref[...] = alpha * l_ref[...] + jnp.sum(psum, axis=1, keepdims=True)
        p = jnp.concatenate([x.astype(BF16) for x in ps], axis=1)
        acc_ref[...] = alpha * acc_ref[...] + _dot(p, vs_ref[0, pl.ds(start, tk), :])
        m_ref[...] = m_new

    sc_ref[0] = scores(tiles_ref[0])

    def step(j, slot):
        sc_ref[1 - slot] = scores(tiles_ref[j + 1])
        softmax_step(sc_ref[slot], tiles_ref[j])

    def pair_body(pj, c):
        step(2 * pj, 0)
        step(2 * pj + 1, 1)
        return c

    lax.fori_loop(0, n_act // 2, pair_body, 0)

    @pl.when(n_act % 2 == 1)
    def _():
        step(n_act - 1, 0)

    softmax_step(jnp.where(dist0 >= last * tk - t0, sc_ref[n_act % 2], NEG), last)
    o_s = acc_ref[...] / l_ref[...]

    gates = jax.nn.sigmoid(gate_ref[0])
    zg = z_ref[0]
    for n in range(hg):
        rs = slice(n * tq, (n + 1) * tq)
        o = (gates[:, n:n + 1] * o_c[rs]
             + gates[:, hg + n:hg + n + 1] * o_s[rs]
             + gates[:, 2 * hg + n:2 * hg + n + 1] * o_w[rs])
        cs = slice(n * LANE, (n + 1) * LANE)
        y_ref[0, :, cs] = (o * _silu(zg[:, cs])).astype(y_ref.dtype)


def _nsa_attention(pb, pf, kvc, b, s, *, tq=128, tk=256):
    g, hg = NSA_GROUPS, NSA_HG
    tq = min(tq, s)
    tk = min(tk, s)
    nsel = s // NSA_SEL_LEN
    ncb = s // NSA_CMP_STRIDE
    r = hg * tq
    assert s % tk == 0 and tk % tq == 0 and tk % LANE == 0 and nsel <= LANE and tk <= NSA_WINDOW
    assert NSA_IND_LANE % (tk // NSA_SEL_LEN) == 0 and tk <= 256
    qbase, kaug, caug = _nsa_constants(s, tq, tk)
    qblocks = NSA_HEADS
    zblocks = NSA_HEADS
    kv_spec = lambda off: pl.BlockSpec((1, s, LANE), lambda bi, gi, i: (bi, 0, qblocks + off * g + gi))
    return pl.pallas_call(
        functools.partial(_nsa_kernel, tq=tq, tk=tk, nsel=nsel),
        grid=(b, g, s // tq),
        in_specs=[
            pl.BlockSpec((1, tq, hg * LANE), lambda bi, gi, i: (bi, i, gi)),
            kv_spec(0), kv_spec(1), kv_spec(2), kv_spec(3),
            pl.BlockSpec((s, LANE), lambda bi, gi, i: (0, 0)),
            pl.BlockSpec((1, 1, ncb, LANE), lambda bi, gi, i: (bi, gi, 0, 0)),
            pl.BlockSpec((1, 1, ncb, LANE), lambda bi, gi, i: (bi, gi, 0, 0)),
            pl.BlockSpec((ncb, LANE), lambda bi, gi, i: (0, 0)),
            pl.BlockSpec((1, r, LANE), lambda bi, gi, i: (gi, 0, 0)),
            pl.BlockSpec((1, tq, LANE), lambda bi, gi, i: (bi, i, zblocks + gi)),
            pl.BlockSpec((1, tq, hg * LANE), lambda bi, gi, i: (bi, i, gi)),
        ],
        out_specs=pl.BlockSpec((1, tq, hg * LANE), lambda bi, gi, i: (bi, i, gi)),
        out_shape=jax.ShapeDtypeStruct((b, s, NSA_HEADS * LANE), BF16),
        scratch_shapes=[
            pltpu.VMEM((2, r, 2 * LANE), BF16),
            pltpu.VMEM((r, LANE), F32), pltpu.VMEM((r, LANE), F32), pltpu.VMEM((r, LANE), F32),
            pltpu.VMEM((2, r, tk), F32),
            pltpu.SMEM((nsel // (tk // NSA_SEL_LEN),), jnp.int32),
            pltpu.SMEM((nsel // (tk // NSA_SEL_LEN) + 1,), jnp.int32),
        ],
        compiler_params=pltpu.CompilerParams(
            dimension_semantics=("arbitrary", "arbitrary", "arbitrary"),
            vmem_limit_bytes=VMEM_LIMIT),
        name="nsa_attention",
    )(pb, pb, pb, pb, pb, kaug, kvc[0], kvc[1], caug, qbase, pf, pf)


def _gmlp_kernel(u_ref, v_ref, z_ref, vg_ref, ws_ref, bst_ref, y_ref, *, n_chunks):
    c = GM_CHUNK
    v = _gelu(v_ref[...])
    ms = jnp.mean(v * v, axis=-1, keepdims=True)
    vn = (v * lax.rsqrt(ms + EPS) * vg_ref[...]).astype(BF16)
    rows = lax.broadcasted_iota(jnp.int32, (c, c), 0)
    cols = lax.broadcasted_iota(jnp.int32, (c, c), 1)
    causal = cols <= rows
    bst = bst_ref[...]
    for gi in range(GM_GROUPS):
        ws = jnp.where(causal, ws_ref[gi], 0.0).astype(BF16)
        cs = slice(gi * LANE, (gi + 1) * LANE)
        for ci in range(n_chunks):
            rs = slice(ci * c, (ci + 1) * c)
            mixed = _dot(ws, vn[rs, cs]) + bst[:, gi:gi + 1]
            y = _gelu(u_ref[rs, cs]) * mixed * _silu(z_ref[rs, cs])
            y_ref[rs, cs] = y.astype(y_ref.dtype)


def _gmlp(pf, vg, ws, bs, *, n_chunks=2):
    n = pf.shape[0]
    w = pf.shape[1] // 3
    tm = GM_CHUNK * n_chunks
    assert n % tm == 0 and w == GM_GROUPS * LANE
    return pl.pallas_call(
        functools.partial(_gmlp_kernel, n_chunks=n_chunks),
        grid=(n // tm,),
        in_specs=[
            pl.BlockSpec((tm, w), lambda i: (i, 0)),
            pl.BlockSpec((tm, w), lambda i: (i, 1)),
            pl.BlockSpec((tm, w), lambda i: (i, 2)),
            pl.BlockSpec((1, w), lambda i: (0, 0)),
            pl.BlockSpec((GM_GROUPS, GM_CHUNK, GM_CHUNK), lambda i: (0, 0, 0)),
            pl.BlockSpec((GM_CHUNK, GM_GROUPS), lambda i: (0, 0)),
        ],
        out_specs=pl.BlockSpec((tm, w), lambda i: (i, 0)),
        out_shape=jax.ShapeDtypeStruct((n, w), BF16),
        compiler_params=pltpu.CompilerParams(
            dimension_semantics=("arbitrary",), vmem_limit_bytes=VMEM_LIMIT),
        name="gmlp_mix",
    )(pf, pf, pf, vg.reshape(1, w), ws, bs.T)


def _sb_layer(x2d, b, s, norm_g, w_in, w_out):
    width = w_in.shape[1] // 4
    scale = np.float32(LANE ** -0.5)
    col_scale = jnp.concatenate([jnp.full((width,), scale, F32), jnp.ones((3 * width,), F32)])
    qkv, zf = _norm_proj(x2d, norm_g, w_in.astype(BF16), col_scale, 3 * width)
    y = _sb_attention(qkv.reshape(b, s, 3 * width), zf.reshape(b, s, width), b, s)
    return y.reshape(b * s, width), w_out.astype(BF16)


def _nsa_layer(x2d, b, s, norm_g, w_in, pos_k, pos_v, ck_w1, ck_w2, cv_w1, cv_w2, w_out):
    d = x2d.shape[1]
    h, g, hg = NSA_HEADS, NSA_GROUPS, NSA_HG
    kvw = g * LANE
    qw = h * LANE
    o_kc, o_vc, o_ks, o_vs, o_kw, o_vw = (qw + a * kvw for a in range(6))
    o_g = qw + 6 * kvw
    o_z = o_g + NSA_BRANCHES * h
    wg = w_in[:, o_g:o_z].reshape(d, NSA_BRANCHES, g, hg).transpose(0, 2, 1, 3)
    wg = wg.reshape(d, g, NSA_BRANCHES * hg)
    wg = jnp.pad(wg, ((0, 0), (0, 0), (0, LANE - NSA_BRANCHES * hg))).reshape(d, g * LANE)
    w_all = jnp.concatenate([
        w_in[:, :qw], w_in[:, o_ks:o_g],
        w_in[:, o_z:], wg, w_in[:, o_kc:o_ks],
    ], axis=1).astype(BF16)
    n1 = qw + 4 * kvw
    scale = np.float32(LANE ** -0.5)
    col_scale = jnp.concatenate([jnp.full((qw,), scale, F32), jnp.ones((w_all.shape[1] - qw,), F32)])
    pb, pf = _norm_proj(x2d, norm_g, w_all, col_scale, n1)
    pf = pf.reshape(b, s, pf.shape[1])
    pb = pb.reshape(b, s, n1)

    pos = jnp.stack([pos_k, pos_v])
    w1 = jnp.stack([ck_w1, cv_w1]).astype(BF16)
    w2 = jnp.stack([ck_w2, cv_w2]).astype(BF16)
    kvc = _nsa_compress(pf, pos, w1, w2, b, s, (qw + g * LANE) // LANE)
    ratio = NSA_SEL_LEN // NSA_CMP_STRIDE
    nsel = s // NSA_SEL_LEN
    kvc = kvc.reshape(2, b, g, nsel, ratio, LANE).transpose(0, 1, 2, 4, 3, 5).reshape(2, b, g, nsel * ratio, LANE)

    y = _nsa_attention(pb, pf, kvc, b, s)
    return y.reshape(b * s, qw), w_out.astype(BF16)


def _gmlp_layer(x2d, norm_g, w_in, v_norm_g, w_s, b_s, w_out):
    ntot = w_in.shape[1]
    _, pf = _norm_proj(x2d, norm_g, w_in.astype(BF16), jnp.ones((ntot,), F32), 0)
    y = _gmlp(pf, v_norm_g, w_s, b_s)
    return y, w_out.astype(BF16)


def kernel(x, p, norm_g, final_norm_g, ple_proj, ple_gate, sb_w_in, sb_w_out, nsa_w_in, nsa_cmp_pos_k, nsa_cmp_pos_v, nsa_cmp_k_w1, nsa_cmp_k_w2, nsa_cmp_v_w1, nsa_cmp_v_w2, nsa_w_out, gm_w_in, gm_v_norm_g, gm_w_s, gm_b_s, gm_w_out):
    b, s, d = x.shape
    depth = p.shape[0]
    x2d = x.reshape(b * s, d)
    for i in range(depth):
        kind, j = i % 3, i // 3
        if kind == 0:
            y, wo = _sb_layer(x2d, b, s, norm_g[i], sb_w_in[j], sb_w_out[j])
        elif kind == 1:
            y, wo = _nsa_layer(x2d, b, s, norm_g[i], nsa_w_in[j], nsa_cmp_pos_k[j], nsa_cmp_pos_v[j],
                               nsa_cmp_k_w1[j], nsa_cmp_k_w2[j], nsa_cmp_v_w1[j], nsa_cmp_v_w2[j],
                               nsa_w_out[j])
        else:
            y, wo = _gmlp_layer(x2d, norm_g[i], gm_w_in[j], gm_v_norm_g[j], gm_w_s[j], gm_b_s[j],
                                gm_w_out[j])
        x2d = _out_ple(y, x2d, p[i].reshape(b * s, p.shape[-1]), wo,
                       ple_gate[i].astype(BF16), ple_proj[i].astype(BF16), final_norm_g,
                       final=(i == depth - 1))
    return x2d.reshape(b, s, d)
```

```python
import functools

import ml_dtypes
import numpy as np
import jax
import jax.numpy as jnp
from jax import lax
from jax.experimental import pallas as pl
from jax.experimental.pallas import tpu as pltpu

F32 = jnp.float32
BF16 = jnp.bfloat16

EPS = 1e-6
NEG = -1e30
BIG = 1e4
LANE = 128
VMEM_LIMIT = 56 * 1024 * 1024

PLE_DIM = 256
SB_HEADS = 16
NSA_HEADS = 16
NSA_GROUPS = 4
NSA_HG = NSA_HEADS // NSA_GROUPS
NSA_CMP_LEN = 32
NSA_CMP_STRIDE = 16
NSA_SEL_LEN = 64
NSA_SEL_TOPK = 16
NSA_WINDOW = 512
NSA_BRANCHES = 3
GM_GROUPS = 16
GM_CHUNK = 128

LOG2E = np.float32(np.log2(np.e))
SB_UNDERFLOW = 110.0
NSA_MASK = 2.0 ** 100
NSA_IND_LANE = LANE // 2


def _dot(a, b):
    return jnp.dot(a, b, preferred_element_type=F32)


def _dot_nt(a, b):
    return lax.dot_general(a, b, (((1,), (1,)), ((), ())), preferred_element_type=F32)


def _gelu(x):
    return 0.5 * x * (1.0 + lax.erf(x * np.float32(1.0 / np.sqrt(2.0))))


def _silu(x):
    return x * jax.nn.sigmoid(x)


def _norm_proj_kernel(x_ref, g_ref, w_ref, cs_ref, *rest, nb1, n_out):
    hn_ref = rest[-1]
    outs = rest[:n_out]
    j = pl.program_id(1)

    @pl.when(j == 0)
    def _():
        xf = x_ref[...]
        ms = jnp.mean(xf * xf, axis=-1, keepdims=True)
        hn_ref[...] = (xf * lax.rsqrt(ms + EPS) * g_ref[...]).astype(BF16)

    def compute():
        return _dot(hn_ref[...], w_ref[...]) * cs_ref[...]

    if n_out == 1:
        outs[0][...] = compute().astype(outs[0].dtype)
    else:
        @pl.when(j < nb1)
        def _():
            outs[0][...] = compute().astype(outs[0].dtype)

        @pl.when(j >= nb1)
        def _():
            outs[1][...] = compute().astype(outs[1].dtype)


def _norm_proj(x2d, g, w, col_scale, n1, *, tm=1024, tn=512):
    n, d = x2d.shape
    ntot = w.shape[1]
    n2 = ntot - n1
    tm = min(tm, n)
    assert n % tm == 0 and n1 % tn == 0 and n2 % tn == 0
    nb1 = n1 // tn
    out_shape, out_specs = [], []
    if n1:
        out_shape.append(jax.ShapeDtypeStruct((n, n1), BF16))
        out_specs.append(pl.BlockSpec((tm, tn), lambda i, j: (i, jnp.minimum(j, nb1 - 1))))
    if n2:
        out_shape.append(jax.ShapeDtypeStruct((n, n2), F32))
        out_specs.append(pl.BlockSpec((tm, tn), lambda i, j: (i, jnp.maximum(j - nb1, 0))))
    outs = pl.pallas_call(
        functools.partial(_norm_proj_kernel, nb1=nb1, n_out=len(out_shape)),
        grid=(n // tm, ntot // tn),
        in_specs=[
            pl.BlockSpec((tm, d), lambda i, j: (i, 0)),
            pl.BlockSpec((1, d), lambda i, j: (0, 0)),
            pl.BlockSpec((d, tn), lambda i, j: (0, j)),
            pl.BlockSpec((1, tn), lambda i, j: (0, j)),
        ],
        out_specs=out_specs,
        out_shape=out_shape,
        scratch_shapes=[pltpu.VMEM((tm, d), BF16)],
        compiler_params=pltpu.CompilerParams(
            dimension_semantics=("arbitrary", "arbitrary"), vmem_limit_bytes=VMEM_LIMIT),
        name="norm_proj",
    )(x2d, g.reshape(1, d), w, col_scale.reshape(1, ntot))
    outs = list(outs)
    o1 = outs.pop(0) if n1 else None
    o2 = outs.pop(0) if n2 else None
    return o1, o2


def _out_ple_kernel(y_ref, x_ref, p_ref, wo_ref, wg_ref, wp_ref, fg_ref, o_ref, *, final):
    x2 = x_ref[...] + _dot(y_ref[...], wo_ref[...])
    gate = jax.nn.sigmoid(_dot(x2.astype(BF16), wg_ref[...]))
    proj = _dot(p_ref[0].astype(BF16), wp_ref[...])
    x3 = x2 + gate * proj
    if final:
        ms = jnp.mean(x3 * x3, axis=-1, keepdims=True)
        x3 = x3 * lax.rsqrt(ms + EPS) * fg_ref[...]
    o_ref[...] = x3


def _out_ple(y, x2d, p3d, layer, wo, wg, wp, fg, *, final, tm=512):
    n, d = x2d.shape
    pd = p3d.shape[2]
    tm = min(tm, n)
    assert n % tm == 0
    const = lambda shape: pl.BlockSpec(shape, lambda i: (0, 0), pipeline_mode=pl.Buffered(1))
    return pl.pallas_call(
        functools.partial(_out_ple_kernel, final=final),
        grid=(n // tm,),
        in_specs=[
            pl.BlockSpec((tm, d), lambda i: (i, 0)),
            pl.BlockSpec((tm, d), lambda i: (i, 0)),
            pl.BlockSpec((1, tm, pd), lambda i: (layer, i, 0)),
            const((d, d)), const((d, d)), const((pd, d)), const((1, d)),
        ],
        out_specs=pl.BlockSpec((tm, d), lambda i: (i, 0)),
        out_shape=jax.ShapeDtypeStruct((n, d), F32),
        compiler_params=pltpu.CompilerParams(
            dimension_semantics=("arbitrary",), vmem_limit_bytes=VMEM_LIMIT),
        name="out_ple",
    )(y, x2d, p3d, wo, wg, wp, fg.reshape(1, d))


def _sb_kernel(q_ref, k_ref, v_ref, z_ref, o_ref, carry_ref, acc_ref, *, tq, hp):
    i = pl.program_id(2)
    rows = lax.broadcasted_iota(jnp.int32, (tq, tq), 0)
    cols = lax.broadcasted_iota(jnp.int32, (tq, tq), 1)
    later = jnp.where(rows > cols, 1.0, 0.0).astype(BF16)
    strictly_causal = cols < rows

    def tile(kt, diag):
        start = pl.multiple_of(kt * tq, tq)
        heads = [slice(hh * LANE, (hh + 1) * LANE) for hh in range(hp)]
        zs = [_dot_nt(q_ref[0, :, cs], k_ref[0, pl.ds(start, tq), cs]) for cs in heads]
        l1ps = [jnp.log1p(jnp.exp(-jnp.abs(z))) for z in zs]
        lms, tails = [], []
        for z, l1p in zip(zs, l1ps):
            sp = jnp.maximum(z, 0.0) + l1p
            lm = jnp.where(strictly_causal, sp, 0.0) if diag else sp
            hi = lm.astype(BF16)
            lo = (lm - hi.astype(F32)).astype(BF16)
            lms.append(lm)
            tails.append(_dot(hi, later) + _dot(lo, later))
        for hh, (z, l1p, lm, tail) in enumerate(zip(zs, l1ps, lms, tails)):
            lsz = jnp.minimum(z, 0.0) - l1p
            carry = jnp.concatenate([carry_ref[hh]] * (tq // LANE), axis=1)
            a = jnp.exp(lsz - tail - carry)
            if diag:
                a = jnp.where(strictly_causal, a, 0.0)
            acc_ref[hh] += _dot(a.astype(BF16), v_ref[0, pl.ds(start, tq), heads[hh]])
            carry_ref[hh] += jnp.sum(lm, axis=1, keepdims=True)

    carry_ref[...] = jnp.zeros_like(carry_ref)
    acc_ref[...] = jnp.zeros_like(acc_ref)
    tile(i, True)

    def cond(st):
        kt, cmin = st
        return jnp.logical_and(kt >= 0, cmin < SB_UNDERFLOW)

    def body(st):
        kt, _ = st
        tile(kt, False)
        return kt - 1, jnp.min(carry_ref[...])

    lax.while_loop(cond, body, (i - 1, jnp.min(carry_ref[...])))
    for hh in range(hp):
        cs = slice(hh * LANE, (hh + 1) * LANE)
        o_ref[0, :, cs] = (acc_ref[hh] * _silu(z_ref[0, :, cs])).astype(o_ref.dtype)


def _sb_attention(qkv, zf, b, s, *, tq=256, hp=4):
    h = SB_HEADS
    tq = min(tq, s)
    assert s % tq == 0 and h % hp == 0
    hb = h // hp
    w = hp * LANE
    return pl.pallas_call(
        functools.partial(_sb_kernel, tq=tq, hp=hp),
        grid=(b, hb, s // tq),
        in_specs=[
            pl.BlockSpec((1, tq, w), lambda bi, hi, i: (bi, i, hi)),
            pl.BlockSpec((1, s, w), lambda bi, hi, i: (bi, 0, hb + hi)),
            pl.BlockSpec((1, s, w), lambda bi, hi, i: (bi, 0, 2 * hb + hi)),
            pl.BlockSpec((1, tq, w), lambda bi, hi, i: (bi, i, hi)),
        ],
        out_specs=pl.BlockSpec((1, tq, w), lambda bi, hi, i: (bi, i, hi)),
        out_shape=jax.ShapeDtypeStruct((b, s, h * LANE), BF16),
        scratch_shapes=[pltpu.VMEM((hp, tq, LANE), F32), pltpu.VMEM((hp, tq, LANE), F32)],
        compiler_params=pltpu.CompilerParams(
            dimension_semantics=("arbitrary", "arbitrary", "arbitrary"),
            vmem_limit_bytes=VMEM_LIMIT),
        name="sb_attention",
    )(qkv, qkv, qkv, zf)


def _cmp_kernel(x_ref, pos_ref, w1_ref, w2_ref, o_ref, *, ncb):
    half = NSA_CMP_LEN // 2
    y1 = jnp.zeros((ncb, LANE), F32)
    y2 = jnp.zeros((ncb, LANE), F32)
    for l in range(NSA_CMP_LEN):
        xl = x_ref[0, pl.ds(l % half, ncb, stride=NSA_CMP_STRIDE), :]
        xl = (xl + pos_ref[0, l:l + 1, :]).astype(BF16)
        part = _dot(xl, w1_ref[0, l * LANE:(l + 1) * LANE, :])
        if l < half:
            y1 = y1 + part
        else:
            y2 = y2 + part
    pre = y1 + pltpu.roll(y2, ncb - 1, 0)
    out = _dot(_gelu(pre).astype(BF16), w2_ref[0])
    valid = lax.broadcasted_iota(jnp.int32, (ncb, LANE), 0) < ncb - 1
    o_ref[0, 0, 0] = jnp.where(valid, out, 0.0).astype(o_ref.dtype)


def _nsa_compress(pf, pos, w1, w2, b, s, col0):
    g = NSA_GROUPS
    ncb = s // NSA_CMP_STRIDE
    return pl.pallas_call(
        functools.partial(_cmp_kernel, ncb=ncb),
        grid=(2, b, g),
        in_specs=[
            pl.BlockSpec((1, s, LANE), lambda kv, bi, gi: (bi, 0, col0 + kv * g + gi)),
            pl.BlockSpec((1, NSA_CMP_LEN, LANE), lambda kv, bi, gi: (kv, 0, 0)),
            pl.BlockSpec((1, NSA_CMP_LEN * LANE, LANE), lambda kv, bi, gi: (kv, 0, 0)),
            pl.BlockSpec((1, LANE, LANE), lambda kv, bi, gi: (kv, 0, 0)),
        ],
        out_specs=pl.BlockSpec((1, 1, 1, ncb, LANE), lambda kv, bi, gi: (kv, bi, gi, 0, 0)),
        out_shape=jax.ShapeDtypeStruct((2, b, g, ncb, LANE), BF16),
        compiler_params=pltpu.CompilerParams(
            dimension_semantics=("arbitrary", "arbitrary", "arbitrary"),
            vmem_limit_bytes=VMEM_LIMIT),
        name="nsa_compress",
    )(pf, pos, w1, w2)


def _split3(x):
    x = np.asarray(x, np.float32)
    bf = ml_dtypes.bfloat16
    hi = x.astype(bf).astype(np.float32)
    mid = (x - hi).astype(bf).astype(np.float32)
    lo = (x - hi - mid).astype(bf).astype(np.float32)
    return hi, mid, lo


def _nsa_constants(s, tq, tk):
    h, g, hg = NSA_HEADS, NSA_GROUPS, NSA_HG
    bf = ml_dtypes.bfloat16
    slopes = np.exp2(-8.0 * np.arange(1, h + 1) / h).astype(np.float32) * LOG2E
    qbase = np.zeros((g, hg * tq, LANE), np.float32)
    for gi in range(g):
        for n in range(hg):
            pieces = np.stack(_split3(slopes[gi * hg + n]))
            qbase[gi, n * tq:(n + 1) * tq, 0:3] = pieces
            qbase[gi, n * tq:(n + 1) * tq, 3:6] = pieces
    pos = np.arange(s)
    kaug = np.zeros((s, LANE), np.float32)
    kaug[:, 0:3] = (pos % tk)[:, None]
    kaug[:, 3:6] = (pos - pos % tk)[:, None]
    kaug[pos, NSA_IND_LANE + (pos // NSA_SEL_LEN) % NSA_IND_LANE] = 1.0
    ratio = NSA_SEL_LEN // NSA_CMP_STRIDE
    nsel = s // NSA_SEL_LEN
    cc = np.arange(s // NSA_CMP_STRIDE)
    cend = (ratio * (cc % nsel) + cc // nsel) * NSA_CMP_STRIDE + (NSA_CMP_LEN - 1)
    caug = np.zeros((cc.size, LANE), np.float32)
    caug[:, 0:3] = (cend % tk)[:, None]
    caug[:, 3:6] = (cend - cend % tk)[:, None]
    return (jnp.asarray(qbase.astype(bf)), jnp.asarray(kaug.astype(bf)), jnp.asarray(caug.astype(bf)))


def _nsa_kernel(q_ref, ks_ref, vs_ref, kw_ref, vw_ref, kaug_ref, kc_ref, vc_ref, caug_ref, qbase_ref,
                gate_ref, z_ref, y_ref, qp_ref, m_ref, l_ref, acc_ref, sc_ref, flag_ref, tiles_ref,
                *, tq, tk, nsel):
    i = pl.program_id(2)
    hg = NSA_HG
    r = hg * tq
    ratio = NSA_SEL_LEN // NSA_CMP_STRIDE
    ncb = nsel * ratio
    bpt = tk // NSA_SEL_LEN
    n_tiles = nsel // bpt
    halves_per_tile = NSA_IND_LANE // bpt
    t0 = i * tq

    qg = q_ref[0]
    qall = jnp.concatenate([qg[:, n * LANE:(n + 1) * LANE] for n in range(hg)], axis=0)
    base = qbase_ref[0]
    qwin = jnp.concatenate([qall, base], axis=1)
    trow = t0 + lax.broadcasted_iota(jnp.int32, (r, 1), 0) % tq
    last = (t0 + tq - 1) // tk
    dist0 = (lax.broadcasted_iota(jnp.int32, (r, tk), 0) % tq
             - lax.broadcasted_iota(jnp.int32, (r, tk), 1))

    n_win = NSA_WINDOW // tk + 1
    win_sc, win_v = [], []
    for j in range(n_win):
        kt = last - (n_win - 1) + j
        start = pl.multiple_of(jnp.maximum(kt, 0) * tk, tk)
        kp = jnp.concatenate([kw_ref[0, pl.ds(start, tk), :], kaug_ref[pl.ds(start, tk), :]], axis=1)
        off = kt * tk - t0
        if j == n_win - 1:
            mask = dist0 >= off
        elif j == 0:
            mask = dist0 < jnp.where(kt >= 0, off + NSA_WINDOW, -tk)
        else:
            mask = kt >= 0
        sc = jnp.where(mask, _dot_nt(qwin, kp), NEG)
        win_sc += [sc[:, c * LANE:(c + 1) * LANE] for c in range(tk // LANE)]
        win_v.append(vw_ref[0, pl.ds(start, tk), :])
    m_w = win_sc[0]
    for ch in win_sc[1:]:
        m_w = jnp.maximum(m_w, ch)
    m_w = jnp.max(m_w, axis=1, keepdims=True)
    win_p = [jnp.exp2(ch - m_w) for ch in win_sc]
    l_w = win_p[0]
    for x in win_p[1:]:
        l_w = l_w + x
    l_w = jnp.sum(l_w, axis=1, keepdims=True)
    cpt = tk // LANE
    o_w = _dot(jnp.concatenate([x.astype(BF16) for x in win_p[:cpt]], axis=1), win_v[0])
    for j in range(1, n_win):
        o_w = o_w + _dot(jnp.concatenate([x.astype(BF16) for x in win_p[j * cpt:(j + 1) * cpt]], axis=1),
                         win_v[j])
    o_w = o_w / l_w

    cc = lax.broadcasted_iota(jnp.int32, (1, ncb), 1)
    cend = (ratio * (cc % nsel) + cc // nsel) * NSA_CMP_STRIDE + (NSA_CMP_LEN - 1)
    mask_c = trow >= cend
    kcp = jnp.concatenate([kc_ref[0, 0], caug_ref[...]], axis=1)
    s_c = jnp.where(mask_c, _dot_nt(qwin, kcp), NEG)
    m_c = jnp.max(s_c, axis=1, keepdims=True)
    e_c = jnp.exp2(s_c - m_c)
    l_c = jnp.sum(e_c, axis=1, keepdims=True)
    p_c = e_c * jnp.where(trow >= NSA_CMP_LEN - 1, 1.0 / l_c, 0.0)
    o_c = _dot(p_c.astype(BF16), vc_ref[0, 0])

    ph = p_c[0:tq]
    for n in range(1, hg):
        ph = ph + p_c[n * tq:(n + 1) * tq]
    imp = ph[:, 0:nsel]
    for a in range(1, ratio):
        imp = imp + ph[:, a * nsel:(a + 1) * nsel]
    if nsel < LANE:
        imp = jnp.concatenate([imp, jnp.zeros((tq, LANE - nsel), F32)], axis=1)

    imp_t = imp.T
    jj = lax.broadcasted_iota(jnp.int32, (LANE, tq), 0)
    cur = (t0 + lax.broadcasted_iota(jnp.int32, (1, tq), 1)) // NSA_SEL_LEN
    forced = (jj == 0) | (jj == cur) | (jj == cur - 1)
    val = jnp.where(forced, BIG, jnp.where(jj <= cur, imp_t, -BIG))
    taken = -3e38
    for _ in range(min(NSA_SEL_TOPK, nsel)):
        vmax = jnp.max(val, axis=0, keepdims=True)
        first = jnp.min(jnp.where(val == vmax, jj, LANE), axis=0, keepdims=True)
        val = jnp.where(jj == first, taken, val)
    sel = jnp.where(val == taken, 1.0, 0.0).T

    col_any = jnp.max(sel, axis=0, keepdims=True)
    for kt in range(n_tiles):
        flag_ref[kt] = (jnp.max(col_any[:, kt * bpt:(kt + 1) * bpt]) > 0.5).astype(jnp.int32)

    lane_q = lax.broadcasted_iota(jnp.int32, (tq, LANE), 1)
    lane_r = lax.broadcasted_iota(jnp.int32, (r, LANE), 1)
    unpicked = jnp.where(sel > 0.5, 0.0, -NSA_MASK)
    for half, part in enumerate((pltpu.roll(unpicked, NSA_IND_LANE, 1), unpicked)):
        part = jnp.where(lane_q >= NSA_IND_LANE, part, 0.0).astype(BF16)
        aug = jnp.where(lane_r >= NSA_IND_LANE, jnp.concatenate([part] * hg, axis=0), base)
        qp_ref[half] = jnp.concatenate([qall, aug], axis=1)

    m_ref[...] = jnp.full_like(m_ref, NEG)
    l_ref[...] = jnp.zeros_like(l_ref)
    acc_ref[...] = jnp.zeros_like(acc_ref)

    def compact(kt, n):
        tiles_ref[n] = kt
        return n + flag_ref[kt]

    n_act = lax.fori_loop(0, last, compact, 0)
    tiles_ref[n_act] = last

    def scores(kt):
        start = pl.multiple_of(kt * tk, tk)
        kp = jnp.concatenate([ks_ref[0, pl.ds(start, tk), :], kaug_ref[pl.ds(start, tk), :]], axis=1)
        return _dot_nt(qp_ref[kt // halves_per_tile], kp)

    def softmax_step(sc, kt):
        start = pl.multiple_of(kt * tk, tk)
        chunks = [sc[:, c * LANE:(c + 1) * LANE] for c in range(tk // LANE)]
        cmax = chunks[0]
        for ch in chunks[1:]:
            cmax = jnp.maximum(cmax, ch)
        m_old = m_ref[...]
        m_new = jnp.maximum(m_old, jnp.max(cmax, axis=1, keepdims=True))
        alpha = jnp.exp2(m_old - m_new)
        ps = [jnp.exp2(ch - m_new) for ch in chunks]
        psum = ps[0]
        for x in ps[1:]:
            psum = psum + x
        l_ref[...] = alpha * l_ref[...] + jnp.sum(psum, axis=1, keepdims=True)
        p = jnp.concatenate([x.astype(BF16) for x in ps], axis=1)
        acc_ref[...] = alpha * acc_ref[...] + _dot(p, vs_ref[0, pl.ds(start, tk), :])
        m_ref[...] = m_new

    sc_ref[0] = scores(tiles_ref[0])

    def step(j, slot):
        sc_ref[1 - slot] = scores(tiles_ref[j + 1])
        softmax_step(sc_ref[slot], tiles_ref[j])

    def pair_body(pj, c):
        step(2 * pj, 0)
        step(2 * pj + 1, 1)
        return c

    lax.fori_loop(0, n_act // 2, pair_body, 0)

    @pl.when(n_act % 2 == 1)
    def _():
        step(n_act - 1, 0)

    softmax_step(jnp.where(dist0 >= last * tk - t0, sc_ref[n_act % 2], NEG), last)
    o_s = acc_ref[...] / l_ref[...]

    gates = jax.nn.sigmoid(gate_ref[0])
    zg = z_ref[0]
    for n in range(hg):
        rs = slice(n * tq, (n + 1) * tq)
        o = (gates[:, n:n + 1] * o_c[rs]
             + gates[:, hg + n:hg + n + 1] * o_s[rs]
             + gates[:, 2 * hg + n:2 * hg + n + 1] * o_w[rs])
        cs = slice(n * LANE, (n + 1) * LANE)
        y_ref[0, :, cs] = (o * _silu(zg[:, cs])).astype(y_ref.dtype)


def _nsa_attention(pb, pf, kvc, b, s, *, tq=128, tk=256):
    g, hg = NSA_GROUPS, NSA_HG
    tq = min(tq, s)
    tk = min(tk, s)
    nsel = s // NSA_SEL_LEN
    ncb = s // NSA_CMP_STRIDE
    r = hg * tq
    assert s % tk == 0 and tk % tq == 0 and tk % LANE == 0 and nsel <= LANE and tk <= NSA_WINDOW
    assert NSA_IND_LANE % (tk // NSA_SEL_LEN) == 0 and tk <= 256
    qbase, kaug, caug = _nsa_constants(s, tq, tk)
    qblocks = NSA_HEADS
    zblocks = NSA_HEADS
    kv_spec = lambda off: pl.BlockSpec((1, s, LANE), lambda bi, gi, i: (bi, 0, qblocks + off * g + gi))
    return pl.pallas_call(
        functools.partial(_nsa_kernel, tq=tq, tk=tk, nsel=nsel),
        grid=(b, g, s // tq),
        in_specs=[
            pl.BlockSpec((1, tq, hg * LANE), lambda bi, gi, i: (bi, i, gi)),
            kv_spec(0), kv_spec(1), kv_spec(2), kv_spec(3),
            pl.BlockSpec((s, LANE), lambda bi, gi, i: (0, 0)),
            pl.BlockSpec((1, 1, ncb, LANE), lambda bi, gi, i: (bi, gi, 0, 0)),
            pl.BlockSpec((1, 1, ncb, LANE), lambda bi, gi, i: (bi, gi, 0, 0)),
            pl.BlockSpec((ncb, LANE), lambda bi, gi, i: (0, 0)),
            pl.BlockSpec((1, r, LANE), lambda bi, gi, i: (gi, 0, 0)),
            pl.BlockSpec((1, tq, LANE), lambda bi, gi, i: (bi, i, zblocks + gi)),
            pl.BlockSpec((1, tq, hg * LANE), lambda bi, gi, i: (bi, i, gi)),
        ],
        out_specs=pl.BlockSpec((1, tq, hg * LANE), lambda bi, gi, i: (bi, i, gi)),
        out_shape=jax.ShapeDtypeStruct((b, s, NSA_HEADS * LANE), BF16),
        scratch_shapes=[
            pltpu.VMEM((2, r, 2 * LANE), BF16),
            pltpu.VMEM((r, LANE), F32), pltpu.VMEM((r, LANE), F32), pltpu.VMEM((r, LANE), F32),
            pltpu.VMEM((2, r, tk), F32),
            pltpu.SMEM((nsel // (tk // NSA_SEL_LEN),), jnp.int32),
            pltpu.SMEM((nsel // (tk // NSA_SEL_LEN) + 1,), jnp.int32),
        ],
        compiler_params=pltpu.CompilerParams(
            dimension_semantics=("arbitrary", "arbitrary", "arbitrary"),
            vmem_limit_bytes=VMEM_LIMIT),
        name="nsa_attention",
    )(pb, pb, pb, pb, pb, kaug, kvc[0], kvc[1], caug, qbase, pf, pf)


def _gmlp_kernel(u_ref, v_ref, z_ref, vg_ref, ws_ref, bst_ref, y_ref, *, n_chunks):
    c = GM_CHUNK
    v = _gelu(v_ref[...])
    ms = jnp.mean(v * v, axis=-1, keepdims=True)
    vn = (v * lax.rsqrt(ms + EPS) * vg_ref[...]).astype(BF16)
    rows = lax.broadcasted_iota(jnp.int32, (c, c), 0)
    cols = lax.broadcasted_iota(jnp.int32, (c, c), 1)
    causal = cols <= rows
    bst = bst_ref[...]
    for gi in range(GM_GROUPS):
        ws = jnp.where(causal, ws_ref[gi], 0.0).astype(BF16)
        cs = slice(gi * LANE, (gi + 1) * LANE)
        for ci in range(n_chunks):
            rs = slice(ci * c, (ci + 1) * c)
            mixed = _dot(ws, vn[rs, cs]) + bst[:, gi:gi + 1]
            y = _gelu(u_ref[rs, cs]) * mixed * _silu(z_ref[rs, cs])
            y_ref[rs, cs] = y.astype(y_ref.dtype)


def _gmlp(pf, vg, ws, bs, *, n_chunks=2):
    n = pf.shape[0]
    w = pf.shape[1] // 3
    tm = GM_CHUNK * n_chunks
    assert n % tm == 0 and w == GM_GROUPS * LANE
    return pl.pallas_call(
        functools.partial(_gmlp_kernel, n_chunks=n_chunks),
        grid=(n // tm,),
        in_specs=[
            pl.BlockSpec((tm, w), lambda i: (i, 0)),
            pl.BlockSpec((tm, w), lambda i: (i, 1)),
            pl.BlockSpec((tm, w), lambda i: (i, 2)),
            pl.BlockSpec((1, w), lambda i: (0, 0)),
            pl.BlockSpec((GM_GROUPS, GM_CHUNK, GM_CHUNK), lambda i: (0, 0, 0)),
            pl.BlockSpec((GM_CHUNK, GM_GROUPS), lambda i: (0, 0)),
        ],
        out_specs=pl.BlockSpec((tm, w), lambda i: (i, 0)),
        out_shape=jax.ShapeDtypeStruct((n, w), BF16),
        compiler_params=pltpu.CompilerParams(
            dimension_semantics=("arbitrary",), vmem_limit_bytes=VMEM_LIMIT),
        name="gmlp_mix",
    )(pf, pf, pf, vg.reshape(1, w), ws, bs.T)


def _sb_layer(x2d, b, s, norm_g, w_in, w_out):
    width = w_in.shape[1] // 4
    scale = np.float32(LANE ** -0.5)
    col_scale = jnp.concatenate([jnp.full((width,), scale, F32), jnp.ones((3 * width,), F32)])
    qkv, zf = _norm_proj(x2d, norm_g, w_in.astype(BF16), col_scale, 3 * width, tn=1024)
    y = _sb_attention(qkv.reshape(b, s, 3 * width), zf.reshape(b, s, width), b, s)
    return y.reshape(b * s, width), w_out.astype(BF16)


def _nsa_layer(x2d, b, s, norm_g, w_in, pos_k, pos_v, ck_w1, ck_w2, cv_w1, cv_w2, w_out):
    d = x2d.shape[1]
    h, g, hg = NSA_HEADS, NSA_GROUPS, NSA_HG
    kvw = g * LANE
    qw = h * LANE
    o_kc, o_vc, o_ks, o_vs, o_kw, o_vw = (qw + a * kvw for a in range(6))
    o_g = qw + 6 * kvw
    o_z = o_g + NSA_BRANCHES * h
    wg = w_in[:, o_g:o_z].reshape(d, NSA_BRANCHES, g, hg).transpose(0, 2, 1, 3)
    wg = wg.reshape(d, g, NSA_BRANCHES * hg)
    wg = jnp.pad(wg, ((0, 0), (0, 0), (0, LANE - NSA_BRANCHES * hg))).reshape(d, g * LANE)
    w_all = jnp.concatenate([
        w_in[:, :qw], w_in[:, o_ks:o_g],
        w_in[:, o_z:], wg, w_in[:, o_kc:o_ks],
    ], axis=1).astype(BF16)
    n1 = qw + 4 * kvw
    scale = np.float32(LANE ** -0.5) * LOG2E
    col_scale = jnp.concatenate([jnp.full((qw,), scale, F32), jnp.ones((w_all.shape[1] - qw,), F32)])
    pb, pf = _norm_proj(x2d, norm_g, w_all, col_scale, n1)
    pf = pf.reshape(b, s, pf.shape[1])
    pb = pb.reshape(b, s, n1)

    pos = jnp.stack([pos_k, pos_v])
    w1 = jnp.stack([ck_w1, cv_w1]).astype(BF16)
    w2 = jnp.stack([ck_w2, cv_w2]).astype(BF16)
    kvc = _nsa_compress(pf, pos, w1, w2, b, s, (qw + g * LANE) // LANE)
    ratio = NSA_SEL_LEN // NSA_CMP_STRIDE
    nsel = s // NSA_SEL_LEN
    kvc = kvc.reshape(2, b, g, nsel, ratio, LANE).transpose(0, 1, 2, 4, 3, 5).reshape(2, b, g, nsel * ratio, LANE)

    y = _nsa_attention(pb, pf, kvc, b, s)
    return y.reshape(b * s, qw), w_out.astype(BF16)


def _gmlp_layer(x2d, norm_g, w_in, v_norm_g, w_s, b_s, w_out):
    ntot = w_in.shape[1]
    _, pf = _norm_proj(x2d, norm_g, w_in.astype(BF16), jnp.ones((ntot,), F32), 0, tn=1024)
    y = _gmlp(pf, v_norm_g, w_s, b_s)
    return y, w_out.astype(BF16)


def kernel(x, p, norm_g, final_norm_g, ple_proj, ple_gate, sb_w_in, sb_w_out, nsa_w_in, nsa_cmp_pos_k, nsa_cmp_pos_v, nsa_cmp_k_w1, nsa_cmp_k_w2, nsa_cmp_v_w1, nsa_cmp_v_w2, nsa_w_out, gm_w_in, gm_v_norm_g, gm_w_s, gm_b_s, gm_w_out):
    b, s, d = x.shape
    depth = p.shape[0]
    x2d = x.reshape(b * s, d)
    for i in range(depth):
        kind, j = i % 3, i // 3
        if kind == 0:
            y, wo = _sb_layer(x2d, b, s, norm_g[i], sb_w_in[j], sb_w_out[j])
        elif kind == 1:
            y, wo = _nsa_layer(x2d, b, s, norm_g[i], nsa_w_in[j], nsa_cmp_pos_k[j], nsa_cmp_pos_v[j],
                               nsa_cmp_k_w1[j], nsa_cmp_k_w2[j], nsa_cmp_v_w1[j], nsa_cmp_v_w2[j],
                               nsa_w_out[j])
        else:
            y, wo = _gmlp_layer(x2d, norm_g[i], gm_w_in[j], gm_v_norm_g[j], gm_w_s[j], gm_b_s[j],
                                gm_w_out[j])
        x2d = _out_ple(y, x2d, p.reshape(depth, b * s, p.shape[-1]), i, wo,
                       ple_gate[i].astype(BF16), ple_proj[i].astype(BF16), final_norm_g,
                       final=(i == depth - 1))
    return x2d.reshape(b, s, d)
```

```python
import functools

import ml_dtypes
import numpy as np
import jax
import jax.numpy as jnp
from jax import lax
from jax.experimental import pallas as pl
from jax.experimental.pallas import tpu as pltpu

F32 = jnp.float32
BF16 = jnp.bfloat16

EPS = 1e-6
NEG = -1e30
BIG = 1e4
LANE = 128
VMEM_LIMIT = 56 * 1024 * 1024

PLE_DIM = 256
SB_HEADS = 16
NSA_HEADS = 16
NSA_GROUPS = 4
NSA_HG = NSA_HEADS // NSA_GROUPS
NSA_CMP_LEN = 32
NSA_CMP_STRIDE = 16
NSA_SEL_LEN = 64
NSA_SEL_TOPK = 16
NSA_WINDOW = 512
NSA_BRANCHES = 3
GM_GROUPS = 16
GM_CHUNK = 128

LOG2E = np.float32(np.log2(np.e))
SB_UNDERFLOW = 110.0
NSA_MASK = 2.0 ** 100
NSA_IND_LANE = LANE // 2


def _dot(a, b):
    return jnp.dot(a, b, preferred_element_type=F32)


def _dot_nt(a, b):
    return lax.dot_general(a, b, (((1,), (1,)), ((), ())), preferred_element_type=F32)


def _gelu(x):
    return 0.5 * x * (1.0 + lax.erf(x * np.float32(1.0 / np.sqrt(2.0))))


def _silu(x):
    return x * jax.nn.sigmoid(x)


def _norm_proj_kernel(x_ref, g_ref, w_ref, cs_ref, *rest, nb1, n_out):
    hn_ref = rest[-1]
    outs = rest[:n_out]
    j = pl.program_id(1)

    @pl.when(j == 0)
    def _():
        xf = x_ref[...]
        ms = jnp.mean(xf * xf, axis=-1, keepdims=True)
        hn_ref[...] = (xf * lax.rsqrt(ms + EPS) * g_ref[...]).astype(BF16)

    def compute():
        return _dot(hn_ref[...], w_ref[...]) * cs_ref[...]

    if n_out == 1:
        outs[0][...] = compute().astype(outs[0].dtype)
    else:
        @pl.when(j < nb1)
        def _():
            outs[0][...] = compute().astype(outs[0].dtype)

        @pl.when(j >= nb1)
        def _():
            outs[1][...] = compute().astype(outs[1].dtype)


def _norm_proj(x2d, g, w, col_scale, n1, *, tm=1024, tn=512):
    n, d = x2d.shape
    ntot = w.shape[1]
    n2 = ntot - n1
    tm = min(tm, n)
    assert n % tm == 0 and n1 % tn == 0 and n2 % tn == 0
    nb1 = n1 // tn
    out_shape, out_specs = [], []
    if n1:
        out_shape.append(jax.ShapeDtypeStruct((n, n1), BF16))
        out_specs.append(pl.BlockSpec((tm, tn), lambda i, j: (i, jnp.minimum(j, nb1 - 1))))
    if n2:
        out_shape.append(jax.ShapeDtypeStruct((n, n2), F32))
        out_specs.append(pl.BlockSpec((tm, tn), lambda i, j: (i, jnp.maximum(j - nb1, 0))))
    outs = pl.pallas_call(
        functools.partial(_norm_proj_kernel, nb1=nb1, n_out=len(out_shape)),
        grid=(n // tm, ntot // tn),
        in_specs=[
            pl.BlockSpec((tm, d), lambda i, j: (i, 0)),
            pl.BlockSpec((1, d), lambda i, j: (0, 0)),
            pl.BlockSpec((d, tn), lambda i, j: (0, j)),
            pl.BlockSpec((1, tn), lambda i, j: (0, j)),
        ],
        out_specs=out_specs,
        out_shape=out_shape,
        scratch_shapes=[pltpu.VMEM((tm, d), BF16)],
        compiler_params=pltpu.CompilerParams(
            dimension_semantics=("arbitrary", "arbitrary"), vmem_limit_bytes=VMEM_LIMIT),
        name="norm_proj",
    )(x2d, g.reshape(1, d), w, col_scale.reshape(1, ntot))
    outs = list(outs)
    o1 = outs.pop(0) if n1 else None
    o2 = outs.pop(0) if n2 else None
    return o1, o2


def _out_ple_kernel(y_ref, x_ref, p_ref, wo_ref, wg_ref, wp_ref, fg_ref, o_ref, *, final):
    x2 = x_ref[...] + _dot(y_ref[...], wo_ref[...])
    gate = jax.nn.sigmoid(_dot(x2.astype(BF16), wg_ref[...]))
    proj = _dot(p_ref[0].astype(BF16), wp_ref[...])
    x3 = x2 + gate * proj
    if final:
        ms = jnp.mean(x3 * x3, axis=-1, keepdims=True)
        x3 = x3 * lax.rsqrt(ms + EPS) * fg_ref[...]
    o_ref[...] = x3


def _out_ple(y, x2d, p3d, layer, wo, wg, wp, fg, *, final, tm=512):
    n, d = x2d.shape
    pd = p3d.shape[2]
    tm = min(tm, n)
    assert n % tm == 0
    const = lambda shape: pl.BlockSpec(shape, lambda i: (0, 0), pipeline_mode=pl.Buffered(1))
    return pl.pallas_call(
        functools.partial(_out_ple_kernel, final=final),
        grid=(n // tm,),
        in_specs=[
            pl.BlockSpec((tm, d), lambda i: (i, 0)),
            pl.BlockSpec((tm, d), lambda i: (i, 0)),
            pl.BlockSpec((1, tm, pd), lambda i: (layer, i, 0)),
            const((d, d)), const((d, d)), const((pd, d)), const((1, d)),
        ],
        out_specs=pl.BlockSpec((tm, d), lambda i: (i, 0)),
        out_shape=jax.ShapeDtypeStruct((n, d), F32),
        compiler_params=pltpu.CompilerParams(
            dimension_semantics=("arbitrary",), vmem_limit_bytes=VMEM_LIMIT),
        name="out_ple",
    )(y, x2d, p3d, wo, wg, wp, fg.reshape(1, d))


def _sb_kernel(q_ref, k_ref, v_ref, z_ref, o_ref, carry_ref, acc_ref, *, tq, hp):
    i = pl.program_id(2)
    rows = lax.broadcasted_iota(jnp.int32, (tq, tq), 0)
    cols = lax.broadcasted_iota(jnp.int32, (tq, tq), 1)
    later = jnp.where(rows > cols, 1.0, 0.0).astype(BF16)
    later2 = jnp.concatenate([later, later], axis=0)
    strictly_causal = cols < rows

    def tile(kt, diag):
        start = pl.multiple_of(kt * tq, tq)
        heads = [slice(hh * LANE, (hh + 1) * LANE) for hh in range(hp)]
        zs = [_dot_nt(q_ref[0, :, cs], k_ref[0, pl.ds(start, tq), cs]) for cs in heads]
        l1ps = [jnp.log2(1.0 + jnp.exp2(-jnp.abs(z))) for z in zs]
        lms, tails = [], []
        for z, l1p in zip(zs, l1ps):
            sp = jnp.maximum(z, 0.0) + l1p
            lm = jnp.where(strictly_causal, sp, 0.0) if diag else sp
            hi = lm.astype(BF16)
            lo = (lm - hi.astype(F32)).astype(BF16)
            lms.append(lm)
            tails.append(_dot(jnp.concatenate([hi, lo], axis=1), later2))
        for hh, (z, l1p, lm, tail) in enumerate(zip(zs, l1ps, lms, tails)):
            lsz = jnp.minimum(z, 0.0) - l1p
            carry = jnp.concatenate([carry_ref[hh]] * (tq // LANE), axis=1)
            a = jnp.exp2(lsz - tail - carry)
            if diag:
                a = jnp.where(strictly_causal, a, 0.0)
            acc_ref[hh] += _dot(a.astype(BF16), v_ref[0, pl.ds(start, tq), heads[hh]])
            carry_ref[hh] += jnp.sum(lm, axis=1, keepdims=True)

    carry_ref[...] = jnp.zeros_like(carry_ref)
    acc_ref[...] = jnp.zeros_like(acc_ref)
    tile(i, True)

    def cond(st):
        kt, cmin = st
        return jnp.logical_and(kt >= 0, cmin < SB_UNDERFLOW * LOG2E)

    def body(st):
        kt, _ = st
        tile(kt, False)
        return kt - 1, jnp.min(carry_ref[...])

    lax.while_loop(cond, body, (i - 1, jnp.min(carry_ref[...])))
    for hh in range(hp):
        cs = slice(hh * LANE, (hh + 1) * LANE)
        o_ref[0, :, cs] = (acc_ref[hh] * _silu(z_ref[0, :, cs])).astype(o_ref.dtype)


def _sb_attention(qkv, zf, b, s, *, tq=256, hp=4):
    h = SB_HEADS
    tq = min(tq, s)
    assert s % tq == 0 and h % hp == 0
    hb = h // hp
    w = hp * LANE
    return pl.pallas_call(
        functools.partial(_sb_kernel, tq=tq, hp=hp),
        grid=(b, hb, s // tq),
        in_specs=[
            pl.BlockSpec((1, tq, w), lambda bi, hi, i: (bi, i, hi)),
            pl.BlockSpec((1, s, w), lambda bi, hi, i: (bi, 0, hb + hi)),
            pl.BlockSpec((1, s, w), lambda bi, hi, i: (bi, 0, 2 * hb + hi)),
            pl.BlockSpec((1, tq, w), lambda bi, hi, i: (bi, i, hi)),
        ],
        out_specs=pl.BlockSpec((1, tq, w), lambda bi, hi, i: (bi, i, hi)),
        out_shape=jax.ShapeDtypeStruct((b, s, h * LANE), BF16),
        scratch_shapes=[pltpu.VMEM((hp, tq, LANE), F32), pltpu.VMEM((hp, tq, LANE), F32)],
        compiler_params=pltpu.CompilerParams(
            dimension_semantics=("arbitrary", "arbitrary", "arbitrary"),
            vmem_limit_bytes=VMEM_LIMIT),
        name="sb_attention",
    )(qkv, qkv, qkv, zf)


def _cmp_kernel(x_ref, pos_ref, w1_ref, w2_ref, o_ref, *, ncb):
    half = NSA_CMP_LEN // 2
    y1 = jnp.zeros((ncb, LANE), F32)
    y2 = jnp.zeros((ncb, LANE), F32)
    for l in range(NSA_CMP_LEN):
        xl = x_ref[0, pl.ds(l % half, ncb, stride=NSA_CMP_STRIDE), :]
        xl = (xl + pos_ref[0, l:l + 1, :]).astype(BF16)
        part = _dot(xl, w1_ref[0, l * LANE:(l + 1) * LANE, :])
        if l < half:
            y1 = y1 + part
        else:
            y2 = y2 + part
    pre = y1 + pltpu.roll(y2, ncb - 1, 0)
    out = _dot(_gelu(pre).astype(BF16), w2_ref[0])
    valid = lax.broadcasted_iota(jnp.int32, (ncb, LANE), 0) < ncb - 1
    o_ref[0, 0, 0] = jnp.where(valid, out, 0.0).astype(o_ref.dtype)


def _nsa_compress(pf, pos, w1, w2, b, s, col0):
    g = NSA_GROUPS
    ncb = s // NSA_CMP_STRIDE
    return pl.pallas_call(
        functools.partial(_cmp_kernel, ncb=ncb),
        grid=(2, b, g),
        in_specs=[
            pl.BlockSpec((1, s, LANE), lambda kv, bi, gi: (bi, 0, col0 + kv * g + gi)),
            pl.BlockSpec((1, NSA_CMP_LEN, LANE), lambda kv, bi, gi: (kv, 0, 0)),
            pl.BlockSpec((1, NSA_CMP_LEN * LANE, LANE), lambda kv, bi, gi: (kv, 0, 0)),
            pl.BlockSpec((1, LANE, LANE), lambda kv, bi, gi: (kv, 0, 0)),
        ],
        out_specs=pl.BlockSpec((1, 1, 1, ncb, LANE), lambda kv, bi, gi: (kv, bi, gi, 0, 0)),
        out_shape=jax.ShapeDtypeStruct((2, b, g, ncb, LANE), BF16),
        compiler_params=pltpu.CompilerParams(
            dimension_semantics=("arbitrary", "arbitrary", "arbitrary"),
            vmem_limit_bytes=VMEM_LIMIT),
        name="nsa_compress",
    )(pf, pos, w1, w2)


def _split3(x):
    x = np.asarray(x, np.float32)
    bf = ml_dtypes.bfloat16
    hi = x.astype(bf).astype(np.float32)
    mid = (x - hi).astype(bf).astype(np.float32)
    lo = (x - hi - mid).astype(bf).astype(np.float32)
    return hi, mid, lo


def _nsa_constants(s, tq, tk):
    h, g, hg = NSA_HEADS, NSA_GROUPS, NSA_HG
    bf = ml_dtypes.bfloat16
    slopes = np.exp2(-8.0 * np.arange(1, h + 1) / h).astype(np.float32) * LOG2E
    qbase = np.zeros((g, hg * tq, LANE), np.float32)
    for gi in range(g):
        for n in range(hg):
            pieces = np.stack(_split3(slopes[gi * hg + n]))
            qbase[gi, n * tq:(n + 1) * tq, 0:3] = pieces
            qbase[gi, n * tq:(n + 1) * tq, 3:6] = pieces
    pos = np.arange(s)
    kaug = np.zeros((s, LANE), np.float32)
    kaug[:, 0:3] = (pos % tk)[:, None]
    kaug[:, 3:6] = (pos - pos % tk)[:, None]
    kaug[pos, NSA_IND_LANE + (pos // NSA_SEL_LEN) % NSA_IND_LANE] = 1.0
    ratio = NSA_SEL_LEN // NSA_CMP_STRIDE
    nsel = s // NSA_SEL_LEN
    cc = np.arange(s // NSA_CMP_STRIDE)
    cend = (ratio * (cc % nsel) + cc // nsel) * NSA_CMP_STRIDE + (NSA_CMP_LEN - 1)
    caug = np.zeros((cc.size, LANE), np.float32)
    caug[:, 0:3] = (cend % tk)[:, None]
    caug[:, 3:6] = (cend - cend % tk)[:, None]
    return (jnp.asarray(qbase.astype(bf)), jnp.asarray(kaug.astype(bf)), jnp.asarray(caug.astype(bf)))


def _nsa_kernel(q_ref, ks_ref, vs_ref, kw_ref, vw_ref, kaug_ref, kc_ref, vc_ref, caug_ref, qbase_ref,
                gate_ref, z_ref, y_ref, qp_ref, m_ref, l_ref, acc_ref, sc_ref, flag_ref, tiles_ref,
                *, tq, tk, nsel):
    i = pl.program_id(2)
    hg = NSA_HG
    r = hg * tq
    ratio = NSA_SEL_LEN // NSA_CMP_STRIDE
    ncb = nsel * ratio
    bpt = tk // NSA_SEL_LEN
    n_tiles = nsel // bpt
    halves_per_tile = NSA_IND_LANE // bpt
    t0 = i * tq

    qg = q_ref[0]
    qall = jnp.concatenate([qg[:, n * LANE:(n + 1) * LANE] for n in range(hg)], axis=0)
    base = qbase_ref[0]
    qwin = jnp.concatenate([qall, base], axis=1)
    trow = t0 + lax.broadcasted_iota(jnp.int32, (r, 1), 0) % tq
    last = (t0 + tq - 1) // tk
    dist0 = (lax.broadcasted_iota(jnp.int32, (r, tk), 0) % tq
             - lax.broadcasted_iota(jnp.int32, (r, tk), 1))

    n_win = NSA_WINDOW // tk + 1
    win_sc, win_v = [], []
    for j in range(n_win):
        kt = last - (n_win - 1) + j
        start = pl.multiple_of(jnp.maximum(kt, 0) * tk, tk)
        kp = jnp.concatenate([kw_ref[0, pl.ds(start, tk), :], kaug_ref[pl.ds(start, tk), :]], axis=1)
        off = kt * tk - t0
        if j == n_win - 1:
            mask = dist0 >= off
        elif j == 0:
            mask = dist0 < jnp.where(kt >= 0, off + NSA_WINDOW, -tk)
        else:
            mask = kt >= 0
        sc = jnp.where(mask, _dot_nt(qwin, kp), NEG)
        win_sc += [sc[:, c * LANE:(c + 1) * LANE] for c in range(tk // LANE)]
        win_v.append(vw_ref[0, pl.ds(start, tk), :])
    m_w = win_sc[0]
    for ch in win_sc[1:]:
        m_w = jnp.maximum(m_w, ch)
    m_w = jnp.max(m_w, axis=1, keepdims=True)
    win_p = [jnp.exp2(ch - m_w) for ch in win_sc]
    l_w = win_p[0]
    for x in win_p[1:]:
        l_w = l_w + x
    l_w = jnp.sum(l_w, axis=1, keepdims=True)
    cpt = tk // LANE
    o_w = _dot(jnp.concatenate([x.astype(BF16) for x in win_p[:cpt]], axis=1), win_v[0])
    for j in range(1, n_win):
        o_w = o_w + _dot(jnp.concatenate([x.astype(BF16) for x in win_p[j * cpt:(j + 1) * cpt]], axis=1),
                         win_v[j])
    o_w = o_w / l_w

    cc = lax.broadcasted_iota(jnp.int32, (1, ncb), 1)
    cend = (ratio * (cc % nsel) + cc // nsel) * NSA_CMP_STRIDE + (NSA_CMP_LEN - 1)
    mask_c = trow >= cend
    kcp = jnp.concatenate([kc_ref[0, 0], caug_ref[...]], axis=1)
    s_c = jnp.where(mask_c, _dot_nt(qwin, kcp), NEG)
    m_c = jnp.max(s_c, axis=1, keepdims=True)
    e_c = jnp.exp2(s_c - m_c)
    l_c = jnp.sum(e_c, axis=1, keepdims=True)
    p_c = e_c * jnp.where(trow >= NSA_CMP_LEN - 1, 1.0 / l_c, 0.0)
    o_c = _dot(p_c.astype(BF16), vc_ref[0, 0])

    ph = p_c[0:tq]
    for n in range(1, hg):
        ph = ph + p_c[n * tq:(n + 1) * tq]
    imp = ph[:, 0:nsel]
    for a in range(1, ratio):
        imp = imp + ph[:, a * nsel:(a + 1) * nsel]
    if nsel < LANE:
        imp = jnp.concatenate([imp, jnp.zeros((tq, LANE - nsel), F32)], axis=1)

    imp_t = imp.T
    jj = lax.broadcasted_iota(jnp.int32, (LANE, tq), 0)
    cur = (t0 + lax.broadcasted_iota(jnp.int32, (1, tq), 1)) // NSA_SEL_LEN
    forced = (jj == 0) | (jj == cur) | (jj == cur - 1)
    val = jnp.where(forced, BIG, jnp.where(jj <= cur, imp_t, -BIG))
    taken = -3e38
    for _ in range(min(NSA_SEL_TOPK, nsel)):
        vmax = jnp.max(val, axis=0, keepdims=True)
        first = jnp.min(jnp.where(val == vmax, jj, LANE), axis=0, keepdims=True)
        val = jnp.where(jj == first, taken, val)
    sel = jnp.where(val == taken, 1.0, 0.0).T

    col_any = jnp.max(sel, axis=0, keepdims=True)
    for kt in range(n_tiles):
        flag_ref[kt] = (jnp.max(col_any[:, kt * bpt:(kt + 1) * bpt]) > 0.5).astype(jnp.int32)

    lane_q = lax.broadcasted_iota(jnp.int32, (tq, LANE), 1)
    lane_r = lax.broadcasted_iota(jnp.int32, (r, LANE), 1)
    unpicked = jnp.where(sel > 0.5, 0.0, -NSA_MASK)
    for half, part in enumerate((pltpu.roll(unpicked, NSA_IND_LANE, 1), unpicked)):
        part = jnp.where(lane_q >= NSA_IND_LANE, part, 0.0).astype(BF16)
        aug = jnp.where(lane_r >= NSA_IND_LANE, jnp.concatenate([part] * hg, axis=0), base)
        qp_ref[half] = jnp.concatenate([qall, aug], axis=1)

    m_ref[...] = jnp.full_like(m_ref, NEG)
    l_ref[...] = jnp.zeros_like(l_ref)
    acc_ref[...] = jnp.zeros_like(acc_ref)

    def compact(kt, n):
        tiles_ref[n] = kt
        return n + flag_ref[kt]

    n_act = lax.fori_loop(0, last, compact, 0)
    tiles_ref[n_act] = last

    def scores(kt):
        start = pl.multiple_of(kt * tk, tk)
        kp = jnp.concatenate([ks_ref[0, pl.ds(start, tk), :], kaug_ref[pl.ds(start, tk), :]], axis=1)
        return _dot_nt(qp_ref[kt // halves_per_tile], kp)

    def softmax_step(sc, kt):
        start = pl.multiple_of(kt * tk, tk)
        chunks = [sc[:, c * LANE:(c + 1) * LANE] for c in range(tk // LANE)]
        cmax = chunks[0]
        for ch in chunks[1:]:
            cmax = jnp.maximum(cmax, ch)
        m_old = m_ref[...]
        m_new = jnp.maximum(m_old, jnp.max(cmax, axis=1, keepdims=True))
        alpha = jnp.exp2(m_old - m_new)
        ps = [jnp.exp2(ch - m_new) for ch in chunks]
        psum = ps[0]
        for x in ps[1:]:
            psum = psum + x
        l_ref[...] = alpha * l_ref[...] + jnp.sum(psum, axis=1, keepdims=True)
        p = jnp.concatenate([x.astype(BF16) for x in ps], axis=1)
        acc_ref[...] = alpha * acc_ref[...] + _dot(p, vs_ref[0, pl.ds(start, tk), :])
        m_ref[...] = m_new

    sc_ref[0] = scores(tiles_ref[0])

    def step(j, slot):
        sc_ref[1 - slot] = scores(tiles_ref[j + 1])
        softmax_step(sc_ref[slot], tiles_ref[j])

    def pair_body(pj, c):
        step(2 * pj, 0)
        step(2 * pj + 1, 1)
        return c

    lax.fori_loop(0, n_act // 2, pair_body, 0)

    @pl.when(n_act % 2 == 1)
    def _():
        step(n_act - 1, 0)

    softmax_step(jnp.where(dist0 >= last * tk - t0, sc_ref[n_act % 2], NEG), last)
    o_s = acc_ref[...] / l_ref[...]

    gates = jax.nn.sigmoid(gate_ref[0])
    zg = z_ref[0]
    for n in range(hg):
        rs = slice(n * tq, (n + 1) * tq)
        o = (gates[:, n:n + 1] * o_c[rs]
             + gates[:, hg + n:hg + n + 1] * o_s[rs]
             + gates[:, 2 * hg + n:2 * hg + n + 1] * o_w[rs])
        cs = slice(n * LANE, (n + 1) * LANE)
        y_ref[0, :, cs] = (o * _silu(zg[:, cs])).astype(y_ref.dtype)


def _nsa_attention(pb, pf, kvc, b, s, *, tq=128, tk=256):
    g, hg = NSA_GROUPS, NSA_HG
    tq = min(tq, s)
    tk = min(tk, s)
    nsel = s // NSA_SEL_LEN
    ncb = s // NSA_CMP_STRIDE
    r = hg * tq
    assert s % tk == 0 and tk % tq == 0 and tk % LANE == 0 and nsel <= LANE and tk <= NSA_WINDOW
    assert NSA_IND_LANE % (tk // NSA_SEL_LEN) == 0 and tk <= 256
    qbase, kaug, caug = _nsa_constants(s, tq, tk)
    qblocks = NSA_HEADS
    zblocks = NSA_HEADS
    kv_spec = lambda off: pl.BlockSpec((1, s, LANE), lambda bi, gi, i: (bi, 0, qblocks + off * g + gi))
    return pl.pallas_call(
        functools.partial(_nsa_kernel, tq=tq, tk=tk, nsel=nsel),
        grid=(b, g, s // tq),
        in_specs=[
            pl.BlockSpec((1, tq, hg * LANE), lambda bi, gi, i: (bi, i, gi)),
            kv_spec(0), kv_spec(1), kv_spec(2), kv_spec(3),
            pl.BlockSpec((s, LANE), lambda bi, gi, i: (0, 0)),
            pl.BlockSpec((1, 1, ncb, LANE), lambda bi, gi, i: (bi, gi, 0, 0)),
            pl.BlockSpec((1, 1, ncb, LANE), lambda bi, gi, i: (bi, gi, 0, 0)),
            pl.BlockSpec((ncb, LANE), lambda bi, gi, i: (0, 0)),
            pl.BlockSpec((1, r, LANE), lambda bi, gi, i: (gi, 0, 0)),
            pl.BlockSpec((1, tq, LANE), lambda bi, gi, i: (bi, i, zblocks + gi)),
            pl.BlockSpec((1, tq, hg * LANE), lambda bi, gi, i: (bi, i, gi)),
        ],
        out_specs=pl.BlockSpec((1, tq, hg * LANE), lambda bi, gi, i: (bi, i, gi)),
        out_shape=jax.ShapeDtypeStruct((b, s, NSA_HEADS * LANE), BF16),
        scratch_shapes=[
            pltpu.VMEM((2, r, 2 * LANE), BF16),
            pltpu.VMEM((r, LANE), F32), pltpu.VMEM((r, LANE), F32), pltpu.VMEM((r, LANE), F32),
            pltpu.VMEM((2, r, tk), F32),
            pltpu.SMEM((nsel // (tk // NSA_SEL_LEN),), jnp.int32),
            pltpu.SMEM((nsel // (tk // NSA_SEL_LEN) + 1,), jnp.int32),
        ],
        compiler_params=pltpu.CompilerParams(
            dimension_semantics=("arbitrary", "arbitrary", "arbitrary"),
            vmem_limit_bytes=VMEM_LIMIT),
        name="nsa_attention",
    )(pb, pb, pb, pb, pb, kaug, kvc[0], kvc[1], caug, qbase, pf, pf)


def _gmlp_kernel(u_ref, v_ref, z_ref, vg_ref, ws_ref, bst_ref, y_ref, *, n_chunks):
    c = GM_CHUNK
    v = _gelu(v_ref[...])
    ms = jnp.mean(v * v, axis=-1, keepdims=True)
    vn = (v * lax.rsqrt(ms + EPS) * vg_ref[...]).astype(BF16)
    rows = lax.broadcasted_iota(jnp.int32, (c, c), 0)
    cols = lax.broadcasted_iota(jnp.int32, (c, c), 1)
    causal = cols <= rows
    bst = bst_ref[...]
    for gi in range(GM_GROUPS):
        ws = jnp.where(causal, ws_ref[gi], 0.0).astype(BF16)
        cs = slice(gi * LANE, (gi + 1) * LANE)
        for ci in range(n_chunks):
            rs = slice(ci * c, (ci + 1) * c)
            mixed = _dot(ws, vn[rs, cs]) + bst[:, gi:gi + 1]
            y = _gelu(u_ref[rs, cs]) * mixed * _silu(z_ref[rs, cs])
            y_ref[rs, cs] = y.astype(y_ref.dtype)


def _gmlp(pf, vg, ws, bs, *, n_chunks=2):
    n = pf.shape[0]
    w = pf.shape[1] // 3
    tm = GM_CHUNK * n_chunks
    assert n % tm == 0 and w == GM_GROUPS * LANE
    return pl.pallas_call(
        functools.partial(_gmlp_kernel, n_chunks=n_chunks),
        grid=(n // tm,),
        in_specs=[
            pl.BlockSpec((tm, w), lambda i: (i, 0)),
            pl.BlockSpec((tm, w), lambda i: (i, 1)),
            pl.BlockSpec((tm, w), lambda i: (i, 2)),
            pl.BlockSpec((1, w), lambda i: (0, 0)),
            pl.BlockSpec((GM_GROUPS, GM_CHUNK, GM_CHUNK), lambda i: (0, 0, 0)),
            pl.BlockSpec((GM_CHUNK, GM_GROUPS), lambda i: (0, 0)),
        ],
        out_specs=pl.BlockSpec((tm, w), lambda i: (i, 0)),
        out_shape=jax.ShapeDtypeStruct((n, w), BF16),
        compiler_params=pltpu.CompilerParams(
            dimension_semantics=("arbitrary",), vmem_limit_bytes=VMEM_LIMIT),
        name="gmlp_mix",
    )(pf, pf, pf, vg.reshape(1, w), ws, bs.T)


def _sb_layer(x2d, b, s, norm_g, w_in, w_out):
    width = w_in.shape[1] // 4
    scale = np.float32(LANE ** -0.5) * LOG2E
    col_scale = jnp.concatenate([jnp.full((width,), scale, F32), jnp.ones((3 * width,), F32)])
    qkv, zf = _norm_proj(x2d, norm_g, w_in.astype(BF16), col_scale, 3 * width, tn=1024)
    y = _sb_attention(qkv.reshape(b, s, 3 * width), zf.reshape(b, s, width), b, s)
    return y.reshape(b * s, width), w_out.astype(BF16)


def _nsa_layer(x2d, b, s, norm_g, w_in, pos_k, pos_v, ck_w1, ck_w2, cv_w1, cv_w2, w_out):
    d = x2d.shape[1]
    h, g, hg = NSA_HEADS, NSA_GROUPS, NSA_HG
    kvw = g * LANE
    qw = h * LANE
    o_kc, o_vc, o_ks, o_vs, o_kw, o_vw = (qw + a * kvw for a in range(6))
    o_g = qw + 6 * kvw
    o_z = o_g + NSA_BRANCHES * h
    wg = w_in[:, o_g:o_z].reshape(d, NSA_BRANCHES, g, hg).transpose(0, 2, 1, 3)
    wg = wg.reshape(d, g, NSA_BRANCHES * hg)
    wg = jnp.pad(wg, ((0, 0), (0, 0), (0, LANE - NSA_BRANCHES * hg))).reshape(d, g * LANE)
    w_all = jnp.concatenate([
        w_in[:, :qw], w_in[:, o_ks:o_g],
        w_in[:, o_z:], wg, w_in[:, o_kc:o_ks],
    ], axis=1).astype(BF16)
    n1 = qw + 4 * kvw
    scale = np.float32(LANE ** -0.5) * LOG2E
    col_scale = jnp.concatenate([jnp.full((qw,), scale, F32), jnp.ones((w_all.shape[1] - qw,), F32)])
    pb, pf = _norm_proj(x2d, norm_g, w_all, col_scale, n1)
    pf = pf.reshape(b, s, pf.shape[1])
    pb = pb.reshape(b, s, n1)

    pos = jnp.stack([pos_k, pos_v])
    w1 = jnp.stack([ck_w1, cv_w1]).astype(BF16)
    w2 = jnp.stack([ck_w2, cv_w2]).astype(BF16)
    kvc = _nsa_compress(pf, pos, w1, w2, b, s, (qw + g * LANE) // LANE)
    ratio = NSA_SEL_LEN // NSA_CMP_STRIDE
    nsel = s // NSA_SEL_LEN
    kvc = kvc.reshape(2, b, g, nsel, ratio, LANE).transpose(0, 1, 2, 4, 3, 5).reshape(2, b, g, nsel * ratio, LANE)

    y = _nsa_attention(pb, pf, kvc, b, s)
    return y.reshape(b * s, qw), w_out.astype(BF16)


def _gmlp_layer(x2d, norm_g, w_in, v_norm_g, w_s, b_s, w_out):
    ntot = w_in.shape[1]
    _, pf = _norm_proj(x2d, norm_g, w_in.astype(BF16), jnp.ones((ntot,), F32), 0, tn=1024)
    y = _gmlp(pf, v_norm_g, w_s, b_s)
    return y, w_out.astype(BF16)


def kernel(x, p, norm_g, final_norm_g, ple_proj, ple_gate, sb_w_in, sb_w_out, nsa_w_in, nsa_cmp_pos_k, nsa_cmp_pos_v, nsa_cmp_k_w1, nsa_cmp_k_w2, nsa_cmp_v_w1, nsa_cmp_v_w2, nsa_w_out, gm_w_in, gm_v_norm_g, gm_w_s, gm_b_s, gm_w_out):
    b, s, d = x.shape
    depth = p.shape[0]
    x2d = x.reshape(b * s, d)
    for i in range(depth):
        kind, j = i % 3, i // 3
        if kind == 0:
            y, wo = _sb_layer(x2d, b, s, norm_g[i], sb_w_in[j], sb_w_out[j])
        elif kind == 1:
            y, wo = _nsa_layer(x2d, b, s, norm_g[i], nsa_w_in[j], nsa_cmp_pos_k[j], nsa_cmp_pos_v[j],
                               nsa_cmp_k_w1[j], nsa_cmp_k_w2[j], nsa_cmp_v_w1[j], nsa_cmp_v_w2[j],
                               nsa_w_out[j])
        else:
            y, wo = _gmlp_layer(x2d, norm_g[i], gm_w_in[j], gm_v_norm_g[j], gm_w_s[j], gm_b_s[j],
                                gm_w_out[j])
        x2d = _out_ple(y, x2d, p.reshape(depth, b * s, p.shape[-1]), i, wo,
                       ple_gate[i].astype(BF16), ple_proj[i].astype(BF16), final_norm_g,
                       final=(i == depth - 1))
    return x2d.reshape(b, s, d)
```

```python
import functools

import ml_dtypes
import numpy as np
import jax
import jax.numpy as jnp
from jax import lax
from jax.experimental import pallas as pl
from jax.experimental.pallas import tpu as pltpu

F32 = jnp.float32
BF16 = jnp.bfloat16

EPS = 1e-6
NEG = -1e30
BIG = 1e4
LANE = 128
VMEM_LIMIT = 56 * 1024 * 1024

PLE_DIM = 256
SB_HEADS = 16
NSA_HEADS = 16
NSA_GROUPS = 4
NSA_HG = NSA_HEADS // NSA_GROUPS
NSA_CMP_LEN = 32
NSA_CMP_STRIDE = 16
NSA_SEL_LEN = 64
NSA_SEL_TOPK = 16
NSA_WINDOW = 512
NSA_BRANCHES = 3
GM_GROUPS = 16
GM_CHUNK = 128

LOG2E = np.float32(np.log2(np.e))
SB_UNDERFLOW = 110.0
NSA_MASK = 2.0 ** 100
NSA_IND_LANE = LANE // 2


def _dot(a, b):
    return jnp.dot(a, b, preferred_element_type=F32)


def _dot_nt(a, b):
    return lax.dot_general(a, b, (((1,), (1,)), ((), ())), preferred_element_type=F32)


def _gelu(x):
    return 0.5 * x * (1.0 + lax.erf(x * np.float32(1.0 / np.sqrt(2.0))))


def _silu(x):
    return x * jax.nn.sigmoid(x)


def _norm_proj_kernel(x_ref, g_ref, w_ref, cs_ref, *rest, nb1, n_out):
    hn_ref = rest[-1]
    outs = rest[:n_out]
    j = pl.program_id(1)

    @pl.when(j == 0)
    def _():
        xf = x_ref[...]
        ms = jnp.mean(xf * xf, axis=-1, keepdims=True)
        hn_ref[...] = (xf * lax.rsqrt(ms + EPS) * g_ref[...]).astype(BF16)

    def compute():
        return _dot(hn_ref[...], w_ref[...]) * cs_ref[...]

    if n_out == 1:
        outs[0][...] = compute().astype(outs[0].dtype)
    else:
        @pl.when(j < nb1)
        def _():
            outs[0][...] = compute().astype(outs[0].dtype)

        @pl.when(j >= nb1)
        def _():
            outs[1][...] = compute().astype(outs[1].dtype)


def _norm_proj(x2d, g, w, col_scale, n1, *, tm=1024, tn=512):
    n, d = x2d.shape
    ntot = w.shape[1]
    n2 = ntot - n1
    tm = min(tm, n)
    assert n % tm == 0 and n1 % tn == 0 and n2 % tn == 0
    nb1 = n1 // tn
    out_shape, out_specs = [], []
    if n1:
        out_shape.append(jax.ShapeDtypeStruct((n, n1), BF16))
        out_specs.append(pl.BlockSpec((tm, tn), lambda i, j: (i, jnp.minimum(j, nb1 - 1))))
    if n2:
        out_shape.append(jax.ShapeDtypeStruct((n, n2), F32))
        out_specs.append(pl.BlockSpec((tm, tn), lambda i, j: (i, jnp.maximum(j - nb1, 0))))
    outs = pl.pallas_call(
        functools.partial(_norm_proj_kernel, nb1=nb1, n_out=len(out_shape)),
        grid=(n // tm, ntot // tn),
        in_specs=[
            pl.BlockSpec((tm, d), lambda i, j: (i, 0)),
            pl.BlockSpec((1, d), lambda i, j: (0, 0)),
            pl.BlockSpec((d, tn), lambda i, j: (0, j)),
            pl.BlockSpec((1, tn), lambda i, j: (0, j)),
        ],
        out_specs=out_specs,
        out_shape=out_shape,
        scratch_shapes=[pltpu.VMEM((tm, d), BF16)],
        compiler_params=pltpu.CompilerParams(
            dimension_semantics=("arbitrary", "arbitrary"), vmem_limit_bytes=VMEM_LIMIT),
        name="norm_proj",
    )(x2d, g.reshape(1, d), w, col_scale.reshape(1, ntot))
    outs = list(outs)
    o1 = outs.pop(0) if n1 else None
    o2 = outs.pop(0) if n2 else None
    return o1, o2


def _out_ple_kernel(y_ref, x_ref, p_ref, wo_ref, wg_ref, wp_ref, fg_ref, o_ref, *, final):
    x2 = x_ref[...] + _dot(y_ref[...], wo_ref[...])
    gate = jax.nn.sigmoid(_dot(x2.astype(BF16), wg_ref[...]))
    proj = _dot(p_ref[0].astype(BF16), wp_ref[...])
    x3 = x2 + gate * proj
    if final:
        ms = jnp.mean(x3 * x3, axis=-1, keepdims=True)
        x3 = x3 * lax.rsqrt(ms + EPS) * fg_ref[...]
    o_ref[...] = x3


def _out_ple(y, x2d, p3d, layer, wo, wg, wp, fg, *, final, tm=512):
    n, d = x2d.shape
    pd = p3d.shape[2]
    tm = min(tm, n)
    assert n % tm == 0
    const = lambda shape: pl.BlockSpec(shape, lambda i: (0, 0), pipeline_mode=pl.Buffered(1))
    return pl.pallas_call(
        functools.partial(_out_ple_kernel, final=final),
        grid=(n // tm,),
        in_specs=[
            pl.BlockSpec((tm, d), lambda i: (i, 0)),
            pl.BlockSpec((tm, d), lambda i: (i, 0)),
            pl.BlockSpec((1, tm, pd), lambda i: (layer, i, 0)),
            const((d, d)), const((d, d)), const((pd, d)), const((1, d)),
        ],
        out_specs=pl.BlockSpec((tm, d), lambda i: (i, 0)),
        out_shape=jax.ShapeDtypeStruct((n, d), F32),
        compiler_params=pltpu.CompilerParams(
            dimension_semantics=("arbitrary",), vmem_limit_bytes=VMEM_LIMIT),
        name="out_ple",
    )(y, x2d, p3d, wo, wg, wp, fg.reshape(1, d))


def _sb_kernel(q_ref, k_ref, v_ref, z_ref, o_ref, carry_ref, acc_ref, *, tq, hp):
    i = pl.program_id(2)
    rows = lax.broadcasted_iota(jnp.int32, (tq, tq), 0)
    cols = lax.broadcasted_iota(jnp.int32, (tq, tq), 1)
    later = jnp.where(rows > cols, 1.0, 0.0).astype(BF16)
    later2 = jnp.concatenate([later, later], axis=0)
    strictly_causal = cols < rows

    def tile(kt, diag):
        start = pl.multiple_of(kt * tq, tq)
        heads = [slice(hh * LANE, (hh + 1) * LANE) for hh in range(hp)]
        zs = [_dot_nt(q_ref[0, :, cs], k_ref[0, pl.ds(start, tq), cs]) for cs in heads]
        l1ps = [jnp.log2(1.0 + jnp.exp2(-jnp.abs(z))) for z in zs]
        lms, tails = [], []
        for z, l1p in zip(zs, l1ps):
            sp = jnp.maximum(z, 0.0) + l1p
            lm = jnp.where(strictly_causal, sp, 0.0) if diag else sp
            hi = lm.astype(BF16)
            lo = (lm - hi.astype(F32)).astype(BF16)
            lms.append(lm)
            tails.append(_dot(jnp.concatenate([hi, lo], axis=1), later2))
        for hh, (z, l1p, lm, tail) in enumerate(zip(zs, l1ps, lms, tails)):
            lsz = jnp.minimum(z, 0.0) - l1p
            carry = jnp.concatenate([carry_ref[hh]] * (tq // LANE), axis=1)
            a = jnp.exp2(lsz - tail - carry)
            if diag:
                a = jnp.where(strictly_causal, a, 0.0)
            acc_ref[hh] += _dot(a.astype(BF16), v_ref[0, pl.ds(start, tq), heads[hh]])
            carry_ref[hh] += jnp.sum(lm, axis=1, keepdims=True)

    carry_ref[...] = jnp.zeros_like(carry_ref)
    acc_ref[...] = jnp.zeros_like(acc_ref)
    tile(i, True)

    def cond(st):
        kt, cmin = st
        return jnp.logical_and(kt >= 0, cmin < SB_UNDERFLOW * LOG2E)

    def body(st):
        kt, _ = st
        tile(kt, False)
        return kt - 1, jnp.min(carry_ref[...])

    lax.while_loop(cond, body, (i - 1, jnp.min(carry_ref[...])))
    for hh in range(hp):
        cs = slice(hh * LANE, (hh + 1) * LANE)
        o_ref[0, :, cs] = (acc_ref[hh] * _silu(z_ref[0, :, cs])).astype(o_ref.dtype)


def _sb_attention(qkv, zf, b, s, *, tq=256, hp=4):
    h = SB_HEADS
    tq = min(tq, s)
    assert s % tq == 0 and h % hp == 0
    hb = h // hp
    w = hp * LANE
    return pl.pallas_call(
        functools.partial(_sb_kernel, tq=tq, hp=hp),
        grid=(b, hb, s // tq),
        in_specs=[
            pl.BlockSpec((1, tq, w), lambda bi, hi, i: (bi, i, hi)),
            pl.BlockSpec((1, s, w), lambda bi, hi, i: (bi, 0, hb + hi)),
            pl.BlockSpec((1, s, w), lambda bi, hi, i: (bi, 0, 2 * hb + hi)),
            pl.BlockSpec((1, tq, w), lambda bi, hi, i: (bi, i, hi)),
        ],
        out_specs=pl.BlockSpec((1, tq, w), lambda bi, hi, i: (bi, i, hi)),
        out_shape=jax.ShapeDtypeStruct((b, s, h * LANE), BF16),
        scratch_shapes=[pltpu.VMEM((hp, tq, LANE), F32), pltpu.VMEM((hp, tq, LANE), F32)],
        compiler_params=pltpu.CompilerParams(
            dimension_semantics=("arbitrary", "arbitrary", "arbitrary"),
            vmem_limit_bytes=VMEM_LIMIT),
        name="sb_attention",
    )(qkv, qkv, qkv, zf)


def _cmp_kernel(x_ref, pos_ref, w1_ref, w2_ref, o_ref, *, ncb):
    half = NSA_CMP_LEN // 2
    y1 = jnp.zeros((ncb, LANE), F32)
    y2 = jnp.zeros((ncb, LANE), F32)
    for l in range(NSA_CMP_LEN):
        xl = x_ref[0, pl.ds(l % half, ncb, stride=NSA_CMP_STRIDE), :]
        xl = (xl + pos_ref[0, l:l + 1, :]).astype(BF16)
        part = _dot(xl, w1_ref[0, l * LANE:(l + 1) * LANE, :])
        if l < half:
            y1 = y1 + part
        else:
            y2 = y2 + part
    pre = y1 + pltpu.roll(y2, ncb - 1, 0)
    out = _dot(_gelu(pre).astype(BF16), w2_ref[0])
    valid = lax.broadcasted_iota(jnp.int32, (ncb, LANE), 0) < ncb - 1
    o_ref[0, 0, 0] = jnp.where(valid, out, 0.0).astype(o_ref.dtype)


def _nsa_compress(pf, pos, w1, w2, b, s, col0):
    g = NSA_GROUPS
    ncb = s // NSA_CMP_STRIDE
    return pl.pallas_call(
        functools.partial(_cmp_kernel, ncb=ncb),
        grid=(2, b, g),
        in_specs=[
            pl.BlockSpec((1, s, LANE), lambda kv, bi, gi: (bi, 0, col0 + kv * g + gi)),
            pl.BlockSpec((1, NSA_CMP_LEN, LANE), lambda kv, bi, gi: (kv, 0, 0)),
            pl.BlockSpec((1, NSA_CMP_LEN * LANE, LANE), lambda kv, bi, gi: (kv, 0, 0)),
            pl.BlockSpec((1, LANE, LANE), lambda kv, bi, gi: (kv, 0, 0)),
        ],
        out_specs=pl.BlockSpec((1, 1, 1, ncb, LANE), lambda kv, bi, gi: (kv, bi, gi, 0, 0)),
        out_shape=jax.ShapeDtypeStruct((2, b, g, ncb, LANE), BF16),
        compiler_params=pltpu.CompilerParams(
            dimension_semantics=("arbitrary", "arbitrary", "arbitrary"),
            vmem_limit_bytes=VMEM_LIMIT),
        name="nsa_compress",
    )(pf, pos, w1, w2)


def _split3(x):
    x = np.asarray(x, np.float32)
    bf = ml_dtypes.bfloat16
    hi = x.astype(bf).astype(np.float32)
    mid = (x - hi).astype(bf).astype(np.float32)
    lo = (x - hi - mid).astype(bf).astype(np.float32)
    return hi, mid, lo


def _nsa_constants(s, tq, tk):
    h, g, hg = NSA_HEADS, NSA_GROUPS, NSA_HG
    bf = ml_dtypes.bfloat16
    slopes = np.exp2(-8.0 * np.arange(1, h + 1) / h).astype(np.float32) * LOG2E
    qbase = np.zeros((g, hg * tq, LANE), np.float32)
    for gi in range(g):
        for n in range(hg):
            pieces = np.stack(_split3(slopes[gi * hg + n]))
            qbase[gi, n * tq:(n + 1) * tq, 0:3] = pieces
            qbase[gi, n * tq:(n + 1) * tq, 3:6] = pieces
    pos = np.arange(s)
    kaug = np.zeros((s, LANE), np.float32)
    kaug[:, 0:3] = (pos % tk)[:, None]
    kaug[:, 3:6] = (pos - pos % tk)[:, None]
    kaug[pos, NSA_IND_LANE + (pos // NSA_SEL_LEN) % NSA_IND_LANE] = 1.0
    ratio = NSA_SEL_LEN // NSA_CMP_STRIDE
    nsel = s // NSA_SEL_LEN
    cc = np.arange(s // NSA_CMP_STRIDE)
    cend = (ratio * (cc % nsel) + cc // nsel) * NSA_CMP_STRIDE + (NSA_CMP_LEN - 1)
    caug = np.zeros((cc.size, LANE), np.float32)
    caug[:, 0:3] = (cend % tk)[:, None]
    caug[:, 3:6] = (cend - cend % tk)[:, None]
    return (jnp.asarray(qbase.astype(bf)), jnp.asarray(kaug.astype(bf)), jnp.asarray(caug.astype(bf)))


def _nsa_kernel(q_ref, ks_ref, vs_ref, kw_ref, vw_ref, kaug_ref, kc_ref, vc_ref, caug_ref, qbase_ref,
                gate_ref, z_ref, y_ref, qp_ref, m_ref, l_ref, acc_ref, sc_ref, flag_ref, tiles_ref,
                *, tq, tk, nsel):
    i = pl.program_id(2)
    hg = NSA_HG
    r = hg * tq
    ratio = NSA_SEL_LEN // NSA_CMP_STRIDE
    ncb = nsel * ratio
    bpt = tk // NSA_SEL_LEN
    n_tiles = nsel // bpt
    halves_per_tile = NSA_IND_LANE // bpt
    t0 = i * tq

    qg = q_ref[0]
    qall = jnp.concatenate([qg[:, n * LANE:(n + 1) * LANE] for n in range(hg)], axis=0)
    base = qbase_ref[0]
    qwin = jnp.concatenate([qall, base], axis=1)
    trow = t0 + lax.broadcasted_iota(jnp.int32, (r, 1), 0) % tq
    last = (t0 + tq - 1) // tk
    dist0 = (lax.broadcasted_iota(jnp.int32, (r, tk), 0) % tq
             - lax.broadcasted_iota(jnp.int32, (r, tk), 1))

    n_win = NSA_WINDOW // tk + 1
    win_sc, win_v = [], []
    for j in range(n_win):
        kt = last - (n_win - 1) + j
        start = pl.multiple_of(jnp.maximum(kt, 0) * tk, tk)
        kp = jnp.concatenate([kw_ref[0, pl.ds(start, tk), :], kaug_ref[pl.ds(start, tk), :]], axis=1)
        off = kt * tk - t0
        if j == n_win - 1:
            mask = dist0 >= off
        elif j == 0:
            mask = dist0 < jnp.where(kt >= 0, off + NSA_WINDOW, -tk)
        else:
            mask = kt >= 0
        sc = jnp.where(mask, _dot_nt(qwin, kp), NEG)
        win_sc += [sc[:, c * LANE:(c + 1) * LANE] for c in range(tk // LANE)]
        win_v.append(vw_ref[0, pl.ds(start, tk), :])
    m_w = win_sc[0]
    for ch in win_sc[1:]:
        m_w = jnp.maximum(m_w, ch)
    m_w = jnp.max(m_w, axis=1, keepdims=True)
    win_p = [jnp.exp2(ch - m_w) for ch in win_sc]
    l_w = win_p[0]
    for x in win_p[1:]:
        l_w = l_w + x
    l_w = jnp.sum(l_w, axis=1, keepdims=True)
    cpt = tk // LANE
    o_w = _dot(jnp.concatenate([x.astype(BF16) for x in win_p[:cpt]], axis=1), win_v[0])
    for j in range(1, n_win):
        o_w = o_w + _dot(jnp.concatenate([x.astype(BF16) for x in win_p[j * cpt:(j + 1) * cpt]], axis=1),
                         win_v[j])
    o_w = o_w / l_w

    cc = lax.broadcasted_iota(jnp.int32, (1, ncb), 1)
    cend = (ratio * (cc % nsel) + cc // nsel) * NSA_CMP_STRIDE + (NSA_CMP_LEN - 1)
    mask_c = trow >= cend
    kcp = jnp.concatenate([kc_ref[0, 0], caug_ref[...]], axis=1)
    s_c = jnp.where(mask_c, _dot_nt(qwin, kcp), NEG)
    m_c = jnp.max(s_c, axis=1, keepdims=True)
    e_c = jnp.exp2(s_c - m_c)
    l_c = jnp.sum(e_c, axis=1, keepdims=True)
    p_c = e_c * jnp.where(trow >= NSA_CMP_LEN - 1, 1.0 / l_c, 0.0)
    o_c = _dot(p_c.astype(BF16), vc_ref[0, 0])

    ph = p_c[0:tq]
    for n in range(1, hg):
        ph = ph + p_c[n * tq:(n + 1) * tq]
    imp = ph[:, 0:nsel]
    for a in range(1, ratio):
        imp = imp + ph[:, a * nsel:(a + 1) * nsel]
    if nsel < LANE:
        imp = jnp.concatenate([imp, jnp.zeros((tq, LANE - nsel), F32)], axis=1)

    imp_t = imp.T
    jj = lax.broadcasted_iota(jnp.int32, (LANE, tq), 0)
    cur = (t0 + lax.broadcasted_iota(jnp.int32, (1, tq), 1)) // NSA_SEL_LEN
    forced = (jj == 0) | (jj == cur) | (jj == cur - 1)
    val = jnp.where(forced, BIG, jnp.where(jj <= cur, imp_t, -BIG))
    taken = -(2.0 ** 127)
    for _ in range(min(NSA_SEL_TOPK, nsel)):
        vmax = jnp.max(val, axis=0, keepdims=True)
        first = jnp.min(jnp.where(val == vmax, jj, LANE), axis=0, keepdims=True)
        val = jnp.where(jj == first, taken, val)
    sel = jnp.where(val == taken, 1.0, 0.0).T

    col_any = jnp.max(sel, axis=0, keepdims=True)
    for kt in range(n_tiles):
        flag_ref[kt] = (jnp.max(col_any[:, kt * bpt:(kt + 1) * bpt]) > 0.5).astype(jnp.int32)

    lane_q = lax.broadcasted_iota(jnp.int32, (tq, LANE), 1)
    lane_r = lax.broadcasted_iota(jnp.int32, (r, LANE), 1)
    unpicked = jnp.where(sel > 0.5, 0.0, -NSA_MASK)
    for half, part in enumerate((pltpu.roll(unpicked, NSA_IND_LANE, 1), unpicked)):
        part = jnp.where(lane_q >= NSA_IND_LANE, part, 0.0).astype(BF16)
        aug = jnp.where(lane_r >= NSA_IND_LANE, jnp.concatenate([part] * hg, axis=0), base)
        qp_ref[half] = jnp.concatenate([qall, aug], axis=1)

    m_ref[...] = jnp.full_like(m_ref, NEG)
    l_ref[...] = jnp.zeros_like(l_ref)
    acc_ref[...] = jnp.zeros_like(acc_ref)

    def compact(kt, n):
        tiles_ref[n] = kt
        return n + flag_ref[kt]

    n_act = lax.fori_loop(0, last, compact, 0)
    tiles_ref[n_act] = last

    def scores(kt):
        start = pl.multiple_of(kt * tk, tk)
        kp = jnp.concatenate([ks_ref[0, pl.ds(start, tk), :], kaug_ref[pl.ds(start, tk), :]], axis=1)
        return _dot_nt(qp_ref[kt // halves_per_tile], kp)

    def softmax_step(sc, kt):
        start = pl.multiple_of(kt * tk, tk)
        chunks = [sc[:, c * LANE:(c + 1) * LANE] for c in range(tk // LANE)]
        cmax = chunks[0]
        for ch in chunks[1:]:
            cmax = jnp.maximum(cmax, ch)
        m_old = m_ref[...]
        m_new = jnp.maximum(m_old, jnp.max(cmax, axis=1, keepdims=True))
        alpha = jnp.exp2(m_old - m_new)
        ps = [jnp.exp2(ch - m_new) for ch in chunks]
        psum = ps[0]
        for x in ps[1:]:
            psum = psum + x
        l_ref[...] = alpha * l_ref[...] + jnp.sum(psum, axis=1, keepdims=True)
        p = jnp.concatenate([x.astype(BF16) for x in ps], axis=1)
        acc_ref[...] = alpha * acc_ref[...] + _dot(p, vs_ref[0, pl.ds(start, tk), :])
        m_ref[...] = m_new

    sc_ref[0] = scores(tiles_ref[0])

    def step(j, slot):
        sc_ref[1 - slot] = scores(tiles_ref[j + 1])
        softmax_step(sc_ref[slot], tiles_ref[j])

    def pair_body(pj, c):
        step(2 * pj, 0)
        step(2 * pj + 1, 1)
        return c

    lax.fori_loop(0, n_act // 2, pair_body, 0)

    @pl.when(n_act % 2 == 1)
    def _():
        step(n_act - 1, 0)

    softmax_step(jnp.where(dist0 >= last * tk - t0, sc_ref[n_act % 2], NEG), last)
    o_s = acc_ref[...] / l_ref[...]

    gates = jax.nn.sigmoid(gate_ref[0])
    zg = z_ref[0]
    for n in range(hg):
        rs = slice(n * tq, (n + 1) * tq)
        o = (gates[:, n:n + 1] * o_c[rs]
             + gates[:, hg + n:hg + n + 1] * o_s[rs]
             + gates[:, 2 * hg + n:2 * hg + n + 1] * o_w[rs])
        cs = slice(n * LANE, (n + 1) * LANE)
        y_ref[0, :, cs] = (o * _silu(zg[:, cs])).astype(y_ref.dtype)


def _nsa_attention(pb, pf, kvc, b, s, *, tq=128, tk=256):
    g, hg = NSA_GROUPS, NSA_HG
    tq = min(tq, s)
    tk = min(tk, s)
    nsel = s // NSA_SEL_LEN
    ncb = s // NSA_CMP_STRIDE
    r = hg * tq
    assert s % tk == 0 and tk % tq == 0 and tk % LANE == 0 and nsel <= LANE and tk <= NSA_WINDOW
    assert NSA_IND_LANE % (tk // NSA_SEL_LEN) == 0 and tk <= 256
    qbase, kaug, caug = _nsa_constants(s, tq, tk)
    qblocks = NSA_HEADS
    zblocks = NSA_HEADS
    kv_spec = lambda off: pl.BlockSpec((1, s, LANE), lambda bi, gi, i: (bi, 0, qblocks + off * g + gi))
    return pl.pallas_call(
        functools.partial(_nsa_kernel, tq=tq, tk=tk, nsel=nsel),
        grid=(b, g, s // tq),
        in_specs=[
            pl.BlockSpec((1, tq, hg * LANE), lambda bi, gi, i: (bi, i, gi)),
            kv_spec(0), kv_spec(1), kv_spec(2), kv_spec(3),
            pl.BlockSpec((s, LANE), lambda bi, gi, i: (0, 0)),
            pl.BlockSpec((1, 1, ncb, LANE), lambda bi, gi, i: (bi, gi, 0, 0)),
            pl.BlockSpec((1, 1, ncb, LANE), lambda bi, gi, i: (bi, gi, 0, 0)),
            pl.BlockSpec((ncb, LANE), lambda bi, gi, i: (0, 0)),
            pl.BlockSpec((1, r, LANE), lambda bi, gi, i: (gi, 0, 0)),
            pl.BlockSpec((1, tq, LANE), lambda bi, gi, i: (bi, i, zblocks + gi)),
            pl.BlockSpec((1, tq, hg * LANE), lambda bi, gi, i: (bi, i, gi)),
        ],
        out_specs=pl.BlockSpec((1, tq, hg * LANE), lambda bi, gi, i: (bi, i, gi)),
        out_shape=jax.ShapeDtypeStruct((b, s, NSA_HEADS * LANE), BF16),
        scratch_shapes=[
            pltpu.VMEM((2, r, 2 * LANE), BF16),
            pltpu.VMEM((r, LANE), F32), pltpu.VMEM((r, LANE), F32), pltpu.VMEM((r, LANE), F32),
            pltpu.VMEM((2, r, tk), F32),
            pltpu.SMEM((nsel // (tk // NSA_SEL_LEN),), jnp.int32),
            pltpu.SMEM((nsel // (tk // NSA_SEL_LEN) + 1,), jnp.int32),
        ],
        compiler_params=pltpu.CompilerParams(
            dimension_semantics=("arbitrary", "arbitrary", "arbitrary"),
            vmem_limit_bytes=VMEM_LIMIT),
        name="nsa_attention",
    )(pb, pb, pb, pb, pb, kaug, kvc[0], kvc[1], caug, qbase, pf, pf)


def _gmlp_kernel(u_ref, v_ref, z_ref, vg_ref, ws_ref, bst_ref, y_ref, *, n_chunks):
    c = GM_CHUNK
    v = _gelu(v_ref[...])
    ms = jnp.mean(v * v, axis=-1, keepdims=True)
    vn = (v * lax.rsqrt(ms + EPS) * vg_ref[...]).astype(BF16)
    rows = lax.broadcasted_iota(jnp.int32, (c, c), 0)
    cols = lax.broadcasted_iota(jnp.int32, (c, c), 1)
    causal = cols <= rows
    bst = bst_ref[...]
    for gi in range(GM_GROUPS):
        ws = jnp.where(causal, ws_ref[gi], 0.0).astype(BF16)
        cs = slice(gi * LANE, (gi + 1) * LANE)
        for ci in range(n_chunks):
            rs = slice(ci * c, (ci + 1) * c)
            mixed = _dot(ws, vn[rs, cs]) + bst[:, gi:gi + 1]
            y = _gelu(u_ref[rs, cs]) * mixed * _silu(z_ref[rs, cs])
            y_ref[rs, cs] = y.astype(y_ref.dtype)


def _gmlp(pf, vg, ws, bs, *, n_chunks=2):
    n = pf.shape[0]
    w = pf.shape[1] // 3
    tm = GM_CHUNK * n_chunks
    assert n % tm == 0 and w == GM_GROUPS * LANE
    return pl.pallas_call(
        functools.partial(_gmlp_kernel, n_chunks=n_chunks),
        grid=(n // tm,),
        in_specs=[
            pl.BlockSpec((tm, w), lambda i: (i, 0)),
            pl.BlockSpec((tm, w), lambda i: (i, 1)),
            pl.BlockSpec((tm, w), lambda i: (i, 2)),
            pl.BlockSpec((1, w), lambda i: (0, 0)),
            pl.BlockSpec((GM_GROUPS, GM_CHUNK, GM_CHUNK), lambda i: (0, 0, 0)),
            pl.BlockSpec((GM_CHUNK, GM_GROUPS), lambda i: (0, 0)),
        ],
        out_specs=pl.BlockSpec((tm, w), lambda i: (i, 0)),
        out_shape=jax.ShapeDtypeStruct((n, w), BF16),
        compiler_params=pltpu.CompilerParams(
            dimension_semantics=("arbitrary",), vmem_limit_bytes=VMEM_LIMIT),
        name="gmlp_mix",
    )(pf, pf, pf, vg.reshape(1, w), ws, bs.T)


def _sb_layer(x2d, b, s, norm_g, w_in, w_out):
    width = w_in.shape[1] // 4
    scale = np.float32(LANE ** -0.5) * LOG2E
    col_scale = jnp.concatenate([jnp.full((width,), scale, F32), jnp.ones((3 * width,), F32)])
    qkv, zf = _norm_proj(x2d, norm_g, w_in.astype(BF16), col_scale, 3 * width, tn=1024)
    y = _sb_attention(qkv.reshape(b, s, 3 * width), zf.reshape(b, s, width), b, s)
    return y.reshape(b * s, width), w_out.astype(BF16)


def _nsa_layer(x2d, b, s, norm_g, w_in, pos_k, pos_v, ck_w1, ck_w2, cv_w1, cv_w2, w_out):
    d = x2d.shape[1]
    h, g, hg = NSA_HEADS, NSA_GROUPS, NSA_HG
    kvw = g * LANE
    qw = h * LANE
    o_kc, o_vc, o_ks, o_vs, o_kw, o_vw = (qw + a * kvw for a in range(6))
    o_g = qw + 6 * kvw
    o_z = o_g + NSA_BRANCHES * h
    wg = w_in[:, o_g:o_z].reshape(d, NSA_BRANCHES, g, hg).transpose(0, 2, 1, 3)
    wg = wg.reshape(d, g, NSA_BRANCHES * hg)
    wg = jnp.pad(wg, ((0, 0), (0, 0), (0, LANE - NSA_BRANCHES * hg))).reshape(d, g * LANE)
    w_all = jnp.concatenate([
        w_in[:, :qw], w_in[:, o_ks:o_g],
        w_in[:, o_z:], wg, w_in[:, o_kc:o_ks],
    ], axis=1).astype(BF16)
    n1 = qw + 4 * kvw
    scale = np.float32(LANE ** -0.5) * LOG2E
    col_scale = jnp.concatenate([jnp.full((qw,), scale, F32), jnp.ones((w_all.shape[1] - qw,), F32)])
    pb, pf = _norm_proj(x2d, norm_g, w_all, col_scale, n1)
    pf = pf.reshape(b, s, pf.shape[1])
    pb = pb.reshape(b, s, n1)

    pos = jnp.stack([pos_k, pos_v])
    w1 = jnp.stack([ck_w1, cv_w1]).astype(BF16)
    w2 = jnp.stack([ck_w2, cv_w2]).astype(BF16)
    kvc = _nsa_compress(pf, pos, w1, w2, b, s, (qw + g * LANE) // LANE)
    ratio = NSA_SEL_LEN // NSA_CMP_STRIDE
    nsel = s // NSA_SEL_LEN
    kvc = kvc.reshape(2, b, g, nsel, ratio, LANE).transpose(0, 1, 2, 4, 3, 5).reshape(2, b, g, nsel * ratio, LANE)

    y = _nsa_attention(pb, pf, kvc, b, s)
    return y.reshape(b * s, qw), w_out.astype(BF16)


def _gmlp_layer(x2d, norm_g, w_in, v_norm_g, w_s, b_s, w_out):
    ntot = w_in.shape[1]
    _, pf = _norm_proj(x2d, norm_g, w_in.astype(BF16), jnp.ones((ntot,), F32), 0, tn=1024)
    y = _gmlp(pf, v_norm_g, w_s, b_s)
    return y, w_out.astype(BF16)


def kernel(x, p, norm_g, final_norm_g, ple_proj, ple_gate, sb_w_in, sb_w_out, nsa_w_in, nsa_cmp_pos_k, nsa_cmp_pos_v, nsa_cmp_k_w1, nsa_cmp_k_w2, nsa_cmp_v_w1, nsa_cmp_v_w2, nsa_w_out, gm_w_in, gm_v_norm_g, gm_w_s, gm_b_s, gm_w_out):
    b, s, d = x.shape
    depth = p.shape[0]
    x2d = x.reshape(b * s, d)
    for i in range(depth):
        kind, j = i % 3, i // 3
        if kind == 0:
            y, wo = _sb_layer(x2d, b, s, norm_g[i], sb_w_in[j], sb_w_out[j])
        elif kind == 1:
            y, wo = _nsa_layer(x2d, b, s, norm_g[i], nsa_w_in[j], nsa_cmp_pos_k[j], nsa_cmp_pos_v[j],
                               nsa_cmp_k_w1[j], nsa_cmp_k_w2[j], nsa_cmp_v_w1[j], nsa_cmp_v_w2[j],
                               nsa_w_out[j])
        else:
            y, wo = _gmlp_layer(x2d, norm_g[i], gm_w_in[j], gm_v_norm_g[j], gm_w_s[j], gm_b_s[j],
                                gm_w_out[j])
        x2d = _out_ple(y, x2d, p.reshape(depth, b * s, p.shape[-1]), i, wo,
                       ple_gate[i].astype(BF16), ple_proj[i].astype(BF16), final_norm_g,
                       final=(i == depth - 1))
    return x2d.reshape(b, s, d)
```

```python
import functools

import ml_dtypes
import numpy as np
import jax
import jax.numpy as jnp
from jax import lax
from jax.experimental import pallas as pl
from jax.experimental.pallas import tpu as pltpu

F32 = jnp.float32
BF16 = jnp.bfloat16

EPS = 1e-6
NEG = -1e30
BIG = 1e4
LANE = 128
VMEM_LIMIT = 56 * 1024 * 1024

PLE_DIM = 256
SB_HEADS = 16
NSA_HEADS = 16
NSA_GROUPS = 4
NSA_HG = NSA_HEADS // NSA_GROUPS
NSA_CMP_LEN = 32
NSA_CMP_STRIDE = 16
NSA_SEL_LEN = 64
NSA_SEL_TOPK = 16
NSA_WINDOW = 512
NSA_BRANCHES = 3
GM_GROUPS = 16
GM_CHUNK = 128

LOG2E = np.float32(np.log2(np.e))
SB_UNDERFLOW = 110.0
NSA_MASK = 2.0 ** 100
NSA_IND_LANE = LANE // 2


def _dot(a, b):
    return jnp.dot(a, b, preferred_element_type=F32)


def _dot_nt(a, b):
    return lax.dot_general(a, b, (((1,), (1,)), ((), ())), preferred_element_type=F32)


def _gelu(x):
    return 0.5 * x * (1.0 + lax.erf(x * np.float32(1.0 / np.sqrt(2.0))))


def _silu(x):
    return x * jax.nn.sigmoid(x)


def _norm_proj_kernel(x_ref, g_ref, w_ref, cs_ref, *rest, nb1, n_out):
    hn_ref = rest[-1]
    outs = rest[:n_out]
    j = pl.program_id(1)

    @pl.when(j == 0)
    def _():
        xf = x_ref[...]
        ms = jnp.mean(xf * xf, axis=-1, keepdims=True)
        hn_ref[...] = (xf * lax.rsqrt(ms + EPS) * g_ref[...]).astype(BF16)

    def compute():
        return _dot(hn_ref[...], w_ref[...]) * cs_ref[...]

    if n_out == 1:
        outs[0][...] = compute().astype(outs[0].dtype)
    else:
        @pl.when(j < nb1)
        def _():
            outs[0][...] = compute().astype(outs[0].dtype)

        @pl.when(j >= nb1)
        def _():
            outs[1][...] = compute().astype(outs[1].dtype)


def _norm_proj(x2d, g, w, col_scale, n1, *, tm=1024, tn=512):
    n, d = x2d.shape
    ntot = w.shape[1]
    n2 = ntot - n1
    tm = min(tm, n)
    assert n % tm == 0 and n1 % tn == 0 and n2 % tn == 0
    nb1 = n1 // tn
    out_shape, out_specs = [], []
    if n1:
        out_shape.append(jax.ShapeDtypeStruct((n, n1), BF16))
        out_specs.append(pl.BlockSpec((tm, tn), lambda i, j: (i, jnp.minimum(j, nb1 - 1))))
    if n2:
        out_shape.append(jax.ShapeDtypeStruct((n, n2), F32))
        out_specs.append(pl.BlockSpec((tm, tn), lambda i, j: (i, jnp.maximum(j - nb1, 0))))
    outs = pl.pallas_call(
        functools.partial(_norm_proj_kernel, nb1=nb1, n_out=len(out_shape)),
        grid=(n // tm, ntot // tn),
        in_specs=[
            pl.BlockSpec((tm, d), lambda i, j: (i, 0)),
            pl.BlockSpec((1, d), lambda i, j: (0, 0)),
            pl.BlockSpec((d, tn), lambda i, j: (0, j)),
            pl.BlockSpec((1, tn), lambda i, j: (0, j)),
        ],
        out_specs=out_specs,
        out_shape=out_shape,
        scratch_shapes=[pltpu.VMEM((tm, d), BF16)],
        compiler_params=pltpu.CompilerParams(
            dimension_semantics=("arbitrary", "arbitrary"), vmem_limit_bytes=VMEM_LIMIT),
        name="norm_proj",
    )(x2d, g.reshape(1, d), w, col_scale.reshape(1, ntot))
    outs = list(outs)
    o1 = outs.pop(0) if n1 else None
    o2 = outs.pop(0) if n2 else None
    return o1, o2


def _out_ple_kernel(y_ref, x_ref, p_ref, wo_ref, wg_ref, wp_ref, fg_ref, o_ref, *, final):
    x2 = x_ref[...] + _dot(y_ref[...], wo_ref[...])
    gate = jax.nn.sigmoid(_dot(x2.astype(BF16), wg_ref[...]))
    proj = _dot(p_ref[0].astype(BF16), wp_ref[...])
    x3 = x2 + gate * proj
    if final:
        ms = jnp.mean(x3 * x3, axis=-1, keepdims=True)
        x3 = x3 * lax.rsqrt(ms + EPS) * fg_ref[...]
    o_ref[...] = x3


def _out_ple(y, x2d, p3d, layer, wo, wg, wp, fg, *, final, tm=512):
    n, d = x2d.shape
    pd = p3d.shape[2]
    tm = min(tm, n)
    assert n % tm == 0
    const = lambda shape: pl.BlockSpec(shape, lambda i: (0, 0), pipeline_mode=pl.Buffered(1))
    return pl.pallas_call(
        functools.partial(_out_ple_kernel, final=final),
        grid=(n // tm,),
        in_specs=[
            pl.BlockSpec((tm, d), lambda i: (i, 0)),
            pl.BlockSpec((tm, d), lambda i: (i, 0)),
            pl.BlockSpec((1, tm, pd), lambda i: (layer, i, 0)),
            const((d, d)), const((d, d)), const((pd, d)), const((1, d)),
        ],
        out_specs=pl.BlockSpec((tm, d), lambda i: (i, 0)),
        out_shape=jax.ShapeDtypeStruct((n, d), F32),
        compiler_params=pltpu.CompilerParams(
            dimension_semantics=("arbitrary",), vmem_limit_bytes=VMEM_LIMIT),
        name="out_ple",
    )(y, x2d, p3d, wo, wg, wp, fg.reshape(1, d))


def _sb_kernel(q_ref, k_ref, v_ref, z_ref, o_ref, carry_ref, acc_ref, *, tq, hp):
    i = pl.program_id(2)
    rows = lax.broadcasted_iota(jnp.int32, (tq, tq), 0)
    cols = lax.broadcasted_iota(jnp.int32, (tq, tq), 1)
    later = jnp.where(rows > cols, 1.0, 0.0).astype(BF16)
    later2 = jnp.concatenate([later, later], axis=0)
    strictly_causal = cols < rows

    def tile(kt, diag):
        start = pl.multiple_of(kt * tq, tq)
        heads = [slice(hh * LANE, (hh + 1) * LANE) for hh in range(hp)]
        zs = [_dot_nt(q_ref[0, :, cs], k_ref[0, pl.ds(start, tq), cs]) for cs in heads]
        l1ps = [jnp.log2(1.0 + jnp.exp2(-jnp.abs(z))) for z in zs]
        lms, tails = [], []
        for z, l1p in zip(zs, l1ps):
            sp = jnp.maximum(z, 0.0) + l1p
            lm = jnp.where(strictly_causal, sp, 0.0) if diag else sp
            hi = lm.astype(BF16)
            lo = (lm - hi.astype(F32)).astype(BF16)
            lms.append(lm)
            tails.append(_dot(jnp.concatenate([hi, lo], axis=1), later2))
        for hh, (z, l1p, lm, tail) in enumerate(zip(zs, l1ps, lms, tails)):
            lsz = jnp.minimum(z, 0.0) - l1p
            carry = jnp.concatenate([carry_ref[hh]] * (tq // LANE), axis=1)
            a = jnp.exp2(lsz - tail - carry)
            if diag:
                a = jnp.where(strictly_causal, a, 0.0)
            acc_ref[hh] += _dot(a.astype(BF16), v_ref[0, pl.ds(start, tq), heads[hh]])
            carry_ref[hh] += jnp.sum(lm, axis=1, keepdims=True)

    carry_ref[...] = jnp.zeros_like(carry_ref)
    acc_ref[...] = jnp.zeros_like(acc_ref)
    tile(i, True)

    def cond(st):
        kt, cmin = st
        return jnp.logical_and(kt >= 0, cmin < SB_UNDERFLOW * LOG2E)

    def body(st):
        kt, _ = st
        tile(kt, False)
        return kt - 1, jnp.min(carry_ref[...])

    lax.while_loop(cond, body, (i - 1, jnp.min(carry_ref[...])))
    for hh in range(hp):
        cs = slice(hh * LANE, (hh + 1) * LANE)
        o_ref[0, :, cs] = (acc_ref[hh] * _silu(z_ref[0, :, cs])).astype(o_ref.dtype)


def _sb_attention(qkv, zf, b, s, *, tq=256, hp=4):
    h = SB_HEADS
    tq = min(tq, s)
    assert s % tq == 0 and h % hp == 0
    hb = h // hp
    w = hp * LANE
    return pl.pallas_call(
        functools.partial(_sb_kernel, tq=tq, hp=hp),
        grid=(b, hb, s // tq),
        in_specs=[
            pl.BlockSpec((1, tq, w), lambda bi, hi, i: (bi, i, hi)),
            pl.BlockSpec((1, s, w), lambda bi, hi, i: (bi, 0, hb + hi)),
            pl.BlockSpec((1, s, w), lambda bi, hi, i: (bi, 0, 2 * hb + hi)),
            pl.BlockSpec((1, tq, w), lambda bi, hi, i: (bi, i, hi)),
        ],
        out_specs=pl.BlockSpec((1, tq, w), lambda bi, hi, i: (bi, i, hi)),
        out_shape=jax.ShapeDtypeStruct((b, s, h * LANE), BF16),
        scratch_shapes=[pltpu.VMEM((hp, tq, LANE), F32), pltpu.VMEM((hp, tq, LANE), F32)],
        compiler_params=pltpu.CompilerParams(
            dimension_semantics=("arbitrary", "arbitrary", "arbitrary"),
            vmem_limit_bytes=VMEM_LIMIT),
        name="sb_attention",
    )(qkv, qkv, qkv, zf)


def _cmp_kernel(x_ref, pos_ref, w1_ref, w2_ref, o_ref, *, ncb):
    half = NSA_CMP_LEN // 2
    y1 = jnp.zeros((ncb, LANE), F32)
    y2 = jnp.zeros((ncb, LANE), F32)
    for l in range(NSA_CMP_LEN):
        xl = x_ref[0, pl.ds(l % half, ncb, stride=NSA_CMP_STRIDE), :]
        xl = (xl + pos_ref[0, l:l + 1, :]).astype(BF16)
        part = _dot(xl, w1_ref[0, l * LANE:(l + 1) * LANE, :])
        if l < half:
            y1 = y1 + part
        else:
            y2 = y2 + part
    pre = y1 + pltpu.roll(y2, ncb - 1, 0)
    out = _dot(_gelu(pre).astype(BF16), w2_ref[0])
    valid = lax.broadcasted_iota(jnp.int32, (ncb, LANE), 0) < ncb - 1
    o_ref[0, 0, 0] = jnp.where(valid, out, 0.0).astype(o_ref.dtype)


def _nsa_compress(pf, pos, w1, w2, b, s, col0):
    g = NSA_GROUPS
    ncb = s // NSA_CMP_STRIDE
    return pl.pallas_call(
        functools.partial(_cmp_kernel, ncb=ncb),
        grid=(2, b, g),
        in_specs=[
            pl.BlockSpec((1, s, LANE), lambda kv, bi, gi: (bi, 0, col0 + kv * g + gi)),
            pl.BlockSpec((1, NSA_CMP_LEN, LANE), lambda kv, bi, gi: (kv, 0, 0)),
            pl.BlockSpec((1, NSA_CMP_LEN * LANE, LANE), lambda kv, bi, gi: (kv, 0, 0)),
            pl.BlockSpec((1, LANE, LANE), lambda kv, bi, gi: (kv, 0, 0)),
        ],
        out_specs=pl.BlockSpec((1, 1, 1, ncb, LANE), lambda kv, bi, gi: (kv, bi, gi, 0, 0)),
        out_shape=jax.ShapeDtypeStruct((2, b, g, ncb, LANE), BF16),
        compiler_params=pltpu.CompilerParams(
            dimension_semantics=("arbitrary", "arbitrary", "arbitrary"),
            vmem_limit_bytes=VMEM_LIMIT),
        name="nsa_compress",
    )(pf, pos, w1, w2)


def _split3(x):
    x = np.asarray(x, np.float32)
    bf = ml_dtypes.bfloat16
    hi = x.astype(bf).astype(np.float32)
    mid = (x - hi).astype(bf).astype(np.float32)
    lo = (x - hi - mid).astype(bf).astype(np.float32)
    return hi, mid, lo


def _nsa_constants(s, tq, tk):
    h, g, hg = NSA_HEADS, NSA_GROUPS, NSA_HG
    bf = ml_dtypes.bfloat16
    slopes = np.exp2(-8.0 * np.arange(1, h + 1) / h).astype(np.float32) * LOG2E
    qbase = np.zeros((g, hg * tq, LANE), np.float32)
    for gi in range(g):
        for n in range(hg):
            pieces = np.stack(_split3(slopes[gi * hg + n]))
            qbase[gi, n * tq:(n + 1) * tq, 0:3] = pieces
            qbase[gi, n * tq:(n + 1) * tq, 3:6] = pieces
    pos = np.arange(s)
    kaug = np.zeros((s, LANE), np.float32)
    kaug[:, 0:3] = (pos % tk)[:, None]
    kaug[:, 3:6] = (pos - pos % tk)[:, None]
    kaug[pos, NSA_IND_LANE + (pos // NSA_SEL_LEN) % NSA_IND_LANE] = 1.0
    ratio = NSA_SEL_LEN // NSA_CMP_STRIDE
    nsel = s // NSA_SEL_LEN
    cc = np.arange(s // NSA_CMP_STRIDE)
    cend = (ratio * (cc % nsel) + cc // nsel) * NSA_CMP_STRIDE + (NSA_CMP_LEN - 1)
    caug = np.zeros((cc.size, LANE), np.float32)
    caug[:, 0:3] = (cend % tk)[:, None]
    caug[:, 3:6] = (cend - cend % tk)[:, None]
    return (jnp.asarray(qbase.astype(bf)), jnp.asarray(kaug.astype(bf)), jnp.asarray(caug.astype(bf)))


def _nsa_kernel(q_ref, ks_ref, vs_ref, kw_ref, vw_ref, kaug_ref, kc_ref, vc_ref, caug_ref, qbase_ref,
                gate_ref, z_ref, y_ref, qp_ref, m_ref, l_ref, acc_ref, sc_ref, flag_ref, tiles_ref,
                *, tq, tk, nsel):
    i = pl.program_id(2)
    hg = NSA_HG
    r = hg * tq
    ratio = NSA_SEL_LEN // NSA_CMP_STRIDE
    ncb = nsel * ratio
    bpt = tk // NSA_SEL_LEN
    n_tiles = nsel // bpt
    halves_per_tile = NSA_IND_LANE // bpt
    t0 = i * tq

    qg = q_ref[0]
    qall = jnp.concatenate([qg[:, n * LANE:(n + 1) * LANE] for n in range(hg)], axis=0)
    base = qbase_ref[0]
    qwin = jnp.concatenate([qall, base], axis=1)
    trow = t0 + lax.broadcasted_iota(jnp.int32, (r, 1), 0) % tq
    last = (t0 + tq - 1) // tk
    dist0 = (lax.broadcasted_iota(jnp.int32, (r, tk), 0) % tq
             - lax.broadcasted_iota(jnp.int32, (r, tk), 1))

    def window_branch():
        n_win = NSA_WINDOW // tk + 1
        win_sc, win_v = [], []
        for j in range(n_win):
            kt = last - (n_win - 1) + j
            start = pl.multiple_of(jnp.maximum(kt, 0) * tk, tk)
            kp = jnp.concatenate([kw_ref[0, pl.ds(start, tk), :], kaug_ref[pl.ds(start, tk), :]], axis=1)
            off = kt * tk - t0
            if j == n_win - 1:
                mask = dist0 >= off
            elif j == 0:
                mask = dist0 < jnp.where(kt >= 0, off + NSA_WINDOW, -tk)
            else:
                mask = kt >= 0
            sc = jnp.where(mask, _dot_nt(qwin, kp), NEG)
            win_sc += [sc[:, c * LANE:(c + 1) * LANE] for c in range(tk // LANE)]
            win_v.append(vw_ref[0, pl.ds(start, tk), :])
        m_w = win_sc[0]
        for ch in win_sc[1:]:
            m_w = jnp.maximum(m_w, ch)
        m_w = jnp.max(m_w, axis=1, keepdims=True)
        win_p = [jnp.exp2(ch - m_w) for ch in win_sc]
        l_w = win_p[0]
        for x in win_p[1:]:
            l_w = l_w + x
        l_w = jnp.sum(l_w, axis=1, keepdims=True)
        cpt = tk // LANE
        o_w = _dot(jnp.concatenate([x.astype(BF16) for x in win_p[:cpt]], axis=1), win_v[0])
        for j in range(1, n_win):
            o_w = o_w + _dot(jnp.concatenate([x.astype(BF16) for x in win_p[j * cpt:(j + 1) * cpt]], axis=1),
                             win_v[j])
        return o_w / l_w

    cc = lax.broadcasted_iota(jnp.int32, (1, ncb), 1)
    cend = (ratio * (cc % nsel) + cc // nsel) * NSA_CMP_STRIDE + (NSA_CMP_LEN - 1)
    mask_c = trow >= cend
    kcp = jnp.concatenate([kc_ref[0, 0], caug_ref[...]], axis=1)
    s_c = jnp.where(mask_c, _dot_nt(qwin, kcp), NEG)
    m_c = jnp.max(s_c, axis=1, keepdims=True)
    e_c = jnp.exp2(s_c - m_c)
    l_c = jnp.sum(e_c, axis=1, keepdims=True)
    p_c = e_c * jnp.where(trow >= NSA_CMP_LEN - 1, 1.0 / l_c, 0.0)
    o_c = _dot(p_c.astype(BF16), vc_ref[0, 0])

    ph = p_c[0:tq]
    for n in range(1, hg):
        ph = ph + p_c[n * tq:(n + 1) * tq]
    imp = ph[:, 0:nsel]
    for a in range(1, ratio):
        imp = imp + ph[:, a * nsel:(a + 1) * nsel]
    if nsel < LANE:
        imp = jnp.concatenate([imp, jnp.zeros((tq, LANE - nsel), F32)], axis=1)

    imp_t = imp.T
    jj = lax.broadcasted_iota(jnp.int32, (LANE, tq), 0)
    cur = (t0 + lax.broadcasted_iota(jnp.int32, (1, tq), 1)) // NSA_SEL_LEN
    forced = (jj == 0) | (jj == cur) | (jj == cur - 1)
    val = jnp.where(forced, BIG, jnp.where(jj <= cur, imp_t, -BIG))
    taken = -(2.0 ** 127)
    for _ in range(min(NSA_SEL_TOPK, nsel)):
        vmax = jnp.max(val, axis=0, keepdims=True)
        first = jnp.min(jnp.where(val == vmax, jj, LANE), axis=0, keepdims=True)
        val = jnp.where(jj == first, taken, val)
    sel = jnp.where(val == taken, 1.0, 0.0).T

    o_w = window_branch()

    col_any = jnp.max(sel, axis=0, keepdims=True)
    for kt in range(n_tiles):
        flag_ref[kt] = (jnp.max(col_any[:, kt * bpt:(kt + 1) * bpt]) > 0.5).astype(jnp.int32)

    lane_q = lax.broadcasted_iota(jnp.int32, (tq, LANE), 1)
    lane_r = lax.broadcasted_iota(jnp.int32, (r, LANE), 1)
    unpicked = jnp.where(sel > 0.5, 0.0, -NSA_MASK)
    for half, part in enumerate((pltpu.roll(unpicked, NSA_IND_LANE, 1), unpicked)):
        part = jnp.where(lane_q >= NSA_IND_LANE, part, 0.0).astype(BF16)
        aug = jnp.where(lane_r >= NSA_IND_LANE, jnp.concatenate([part] * hg, axis=0), base)
        qp_ref[half] = jnp.concatenate([qall, aug], axis=1)

    m_ref[...] = jnp.full_like(m_ref, NEG)
    l_ref[...] = jnp.zeros_like(l_ref)
    acc_ref[...] = jnp.zeros_like(acc_ref)

    def compact(kt, n):
        tiles_ref[n] = kt
        return n + flag_ref[kt]

    n_act = lax.fori_loop(0, last, compact, 0)
    tiles_ref[n_act] = last

    def scores(kt):
        start = pl.multiple_of(kt * tk, tk)
        kp = jnp.concatenate([ks_ref[0, pl.ds(start, tk), :], kaug_ref[pl.ds(start, tk), :]], axis=1)
        return _dot_nt(qp_ref[kt // halves_per_tile], kp)

    def softmax_step(sc, kt):
        start = pl.multiple_of(kt * tk, tk)
        chunks = [sc[:, c * LANE:(c + 1) * LANE] for c in range(tk // LANE)]
        cmax = chunks[0]
        for ch in chunks[1:]:
            cmax = jnp.maximum(cmax, ch)
        m_old = m_ref[...]
        m_new = jnp.maximum(m_old, jnp.max(cmax, axis=1, keepdims=True))
        alpha = jnp.exp2(m_old - m_new)
        ps = [jnp.exp2(ch - m_new) for ch in chunks]
        psum = ps[0]
        for x in ps[1:]:
            psum = psum + x
        l_ref[...] = alpha * l_ref[...] + jnp.sum(psum, axis=1, keepdims=True)
        p = jnp.concatenate([x.astype(BF16) for x in ps], axis=1)
        acc_ref[...] = alpha * acc_ref[...] + _dot(p, vs_ref[0, pl.ds(start, tk), :])
        m_ref[...] = m_new

    sc_ref[0] = scores(tiles_ref[0])

    def step(j, slot):
        sc_ref[1 - slot] = scores(tiles_ref[j + 1])
        softmax_step(sc_ref[slot], tiles_ref[j])

    def pair_body(pj, c):
        step(2 * pj, 0)
        step(2 * pj + 1, 1)
        return c

    lax.fori_loop(0, n_act // 2, pair_body, 0)

    @pl.when(n_act % 2 == 1)
    def _():
        step(n_act - 1, 0)

    softmax_step(jnp.where(dist0 >= last * tk - t0, sc_ref[n_act % 2], NEG), last)
    o_s = acc_ref[...] / l_ref[...]

    gates = jax.nn.sigmoid(gate_ref[0])
    zg = z_ref[0]
    for n in range(hg):
        rs = slice(n * tq, (n + 1) * tq)
        o = (gates[:, n:n + 1] * o_c[rs]
             + gates[:, hg + n:hg + n + 1] * o_s[rs]
             + gates[:, 2 * hg + n:2 * hg + n + 1] * o_w[rs])
        cs = slice(n * LANE, (n + 1) * LANE)
        y_ref[0, :, cs] = (o * _silu(zg[:, cs])).astype(y_ref.dtype)


def _nsa_attention(pb, pf, kvc, b, s, *, tq=128, tk=256):
    g, hg = NSA_GROUPS, NSA_HG
    tq = min(tq, s)
    tk = min(tk, s)
    nsel = s // NSA_SEL_LEN
    ncb = s // NSA_CMP_STRIDE
    r = hg * tq
    assert s % tk == 0 and tk % tq == 0 and tk % LANE == 0 and nsel <= LANE and tk <= NSA_WINDOW
    assert NSA_IND_LANE % (tk // NSA_SEL_LEN) == 0 and tk <= 256
    qbase, kaug, caug = _nsa_constants(s, tq, tk)
    qblocks = NSA_HEADS
    zblocks = NSA_HEADS
    kv_spec = lambda off: pl.BlockSpec((1, s, LANE), lambda bi, gi, i: (bi, 0, qblocks + off * g + gi))
    return pl.pallas_call(
        functools.partial(_nsa_kernel, tq=tq, tk=tk, nsel=nsel),
        grid=(b, g, s // tq),
        in_specs=[
            pl.BlockSpec((1, tq, hg * LANE), lambda bi, gi, i: (bi, i, gi)),
            kv_spec(0), kv_spec(1), kv_spec(2), kv_spec(3),
            pl.BlockSpec((s, LANE), lambda bi, gi, i: (0, 0)),
            pl.BlockSpec((1, 1, ncb, LANE), lambda bi, gi, i: (bi, gi, 0, 0)),
            pl.BlockSpec((1, 1, ncb, LANE), lambda bi, gi, i: (bi, gi, 0, 0)),
            pl.BlockSpec((ncb, LANE), lambda bi, gi, i: (0, 0)),
            pl.BlockSpec((1, r, LANE), lambda bi, gi, i: (gi, 0, 0)),
            pl.BlockSpec((1, tq, LANE), lambda bi, gi, i: (bi, i, zblocks + gi)),
            pl.BlockSpec((1, tq, hg * LANE), lambda bi, gi, i: (bi, i, gi)),
        ],
        out_specs=pl.BlockSpec((1, tq, hg * LANE), lambda bi, gi, i: (bi, i, gi)),
        out_shape=jax.ShapeDtypeStruct((b, s, NSA_HEADS * LANE), BF16),
        scratch_shapes=[
            pltpu.VMEM((2, r, 2 * LANE), BF16),
            pltpu.VMEM((r, LANE), F32), pltpu.VMEM((r, LANE), F32), pltpu.VMEM((r, LANE), F32),
            pltpu.VMEM((2, r, tk), F32),
            pltpu.SMEM((nsel // (tk // NSA_SEL_LEN),), jnp.int32),
            pltpu.SMEM((nsel // (tk // NSA_SEL_LEN) + 1,), jnp.int32),
        ],
        compiler_params=pltpu.CompilerParams(
            dimension_semantics=("arbitrary", "arbitrary", "arbitrary"),
            vmem_limit_bytes=VMEM_LIMIT),
        name="nsa_attention",
    )(pb, pb, pb, pb, pb, kaug, kvc[0], kvc[1], caug, qbase, pf, pf)


def _gmlp_kernel(u_ref, v_ref, z_ref, vg_ref, ws_ref, bst_ref, y_ref, *, n_chunks):
    c = GM_CHUNK
    v = _gelu(v_ref[...])
    ms = jnp.mean(v * v, axis=-1, keepdims=True)
    vn = (v * lax.rsqrt(ms + EPS) * vg_ref[...]).astype(BF16)
    rows = lax.broadcasted_iota(jnp.int32, (c, c), 0)
    cols = lax.broadcasted_iota(jnp.int32, (c, c), 1)
    causal = cols <= rows
    bst = bst_ref[...]
    for gi in range(GM_GROUPS):
        ws = jnp.where(causal, ws_ref[gi], 0.0).astype(BF16)
        cs = slice(gi * LANE, (gi + 1) * LANE)
        for ci in range(n_chunks):
            rs = slice(ci * c, (ci + 1) * c)
            mixed = _dot(ws, vn[rs, cs]) + bst[:, gi:gi + 1]
            y = _gelu(u_ref[rs, cs]) * mixed * _silu(z_ref[rs, cs])
            y_ref[rs, cs] = y.astype(y_ref.dtype)


def _gmlp(pf, vg, ws, bs, *, n_chunks=2):
    n = pf.shape[0]
    w = pf.shape[1] // 3
    tm = GM_CHUNK * n_chunks
    assert n % tm == 0 and w == GM_GROUPS * LANE
    return pl.pallas_call(
        functools.partial(_gmlp_kernel, n_chunks=n_chunks),
        grid=(n // tm,),
        in_specs=[
            pl.BlockSpec((tm, w), lambda i: (i, 0)),
            pl.BlockSpec((tm, w), lambda i: (i, 1)),
            pl.BlockSpec((tm, w), lambda i: (i, 2)),
            pl.BlockSpec((1, w), lambda i: (0, 0)),
            pl.BlockSpec((GM_GROUPS, GM_CHUNK, GM_CHUNK), lambda i: (0, 0, 0)),
            pl.BlockSpec((GM_CHUNK, GM_GROUPS), lambda i: (0, 0)),
        ],
        out_specs=pl.BlockSpec((tm, w), lambda i: (i, 0)),
        out_shape=jax.ShapeDtypeStruct((n, w), BF16),
        compiler_params=pltpu.CompilerParams(
            dimension_semantics=("arbitrary",), vmem_limit_bytes=VMEM_LIMIT),
        name="gmlp_mix",
    )(pf, pf, pf, vg.reshape(1, w), ws, bs.T)


def _sb_layer(x2d, b, s, norm_g, w_in, w_out):
    width = w_in.shape[1] // 4
    scale = np.float32(LANE ** -0.5) * LOG2E
    col_scale = jnp.concatenate([jnp.full((width,), scale, F32), jnp.ones((3 * width,), F32)])
    qkv, zf = _norm_proj(x2d, norm_g, w_in.astype(BF16), col_scale, 3 * width, tn=1024)
    y = _sb_attention(qkv.reshape(b, s, 3 * width), zf.reshape(b, s, width), b, s)
    return y.reshape(b * s, width), w_out.astype(BF16)


def _nsa_layer(x2d, b, s, norm_g, w_in, pos_k, pos_v, ck_w1, ck_w2, cv_w1, cv_w2, w_out):
    d = x2d.shape[1]
    h, g, hg = NSA_HEADS, NSA_GROUPS, NSA_HG
    kvw = g * LANE
    qw = h * LANE
    o_kc, o_vc, o_ks, o_vs, o_kw, o_vw = (qw + a * kvw for a in range(6))
    o_g = qw + 6 * kvw
    o_z = o_g + NSA_BRANCHES * h
    wg = w_in[:, o_g:o_z].reshape(d, NSA_BRANCHES, g, hg).transpose(0, 2, 1, 3)
    wg = wg.reshape(d, g, NSA_BRANCHES * hg)
    wg = jnp.pad(wg, ((0, 0), (0, 0), (0, LANE - NSA_BRANCHES * hg))).reshape(d, g * LANE)
    w_all = jnp.concatenate([
        w_in[:, :qw], w_in[:, o_ks:o_g],
        w_in[:, o_z:], wg, w_in[:, o_kc:o_ks],
    ], axis=1).astype(BF16)
    n1 = qw + 4 * kvw
    scale = np.float32(LANE ** -0.5) * LOG2E
    col_scale = jnp.concatenate([jnp.full((qw,), scale, F32), jnp.ones((w_all.shape[1] - qw,), F32)])
    pb, pf = _norm_proj(x2d, norm_g, w_all, col_scale, n1)
    pf = pf.reshape(b, s, pf.shape[1])
    pb = pb.reshape(b, s, n1)

    pos = jnp.stack([pos_k, pos_v])
    w1 = jnp.stack([ck_w1, cv_w1]).astype(BF16)
    w2 = jnp.stack([ck_w2, cv_w2]).astype(BF16)
    kvc = _nsa_compress(pf, pos, w1, w2, b, s, (qw + g * LANE) // LANE)
    ratio = NSA_SEL_LEN // NSA_CMP_STRIDE
    nsel = s // NSA_SEL_LEN
    kvc = kvc.reshape(2, b, g, nsel, ratio, LANE).transpose(0, 1, 2, 4, 3, 5).reshape(2, b, g, nsel * ratio, LANE)

    y = _nsa_attention(pb, pf, kvc, b, s)
    return y.reshape(b * s, qw), w_out.astype(BF16)


def _gmlp_layer(x2d, norm_g, w_in, v_norm_g, w_s, b_s, w_out):
    ntot = w_in.shape[1]
    _, pf = _norm_proj(x2d, norm_g, w_in.astype(BF16), jnp.ones((ntot,), F32), 0, tn=1024)
    y = _gmlp(pf, v_norm_g, w_s, b_s)
    return y, w_out.astype(BF16)


def kernel(x, p, norm_g, final_norm_g, ple_proj, ple_gate, sb_w_in, sb_w_out, nsa_w_in, nsa_cmp_pos_k, nsa_cmp_pos_v, nsa_cmp_k_w1, nsa_cmp_k_w2, nsa_cmp_v_w1, nsa_cmp_v_w2, nsa_w_out, gm_w_in, gm_v_norm_g, gm_w_s, gm_b_s, gm_w_out):
    b, s, d = x.shape
    depth = p.shape[0]
    x2d = x.reshape(b * s, d)
    for i in range(depth):
        kind, j = i % 3, i // 3
        if kind == 0:
            y, wo = _sb_layer(x2d, b, s, norm_g[i], sb_w_in[j], sb_w_out[j])
        elif kind == 1:
            y, wo = _nsa_layer(x2d, b, s, norm_g[i], nsa_w_in[j], nsa_cmp_pos_k[j], nsa_cmp_pos_v[j],
                               nsa_cmp_k_w1[j], nsa_cmp_k_w2[j], nsa_cmp_v_w1[j], nsa_cmp_v_w2[j],
                               nsa_w_out[j])
        else:
            y, wo = _gmlp_layer(x2d, norm_g[i], gm_w_in[j], gm_v_norm_g[j], gm_w_s[j], gm_b_s[j],
                                gm_w_out[j])
        x2d = _out_ple(y, x2d, p.reshape(depth, b * s, p.shape[-1]), i, wo,
                       ple_gate[i].astype(BF16), ple_proj[i].astype(BF16), final_norm_g,
                       final=(i == depth - 1))
    return x2d.reshape(b, s, d)
```

```python
import functools

import ml_dtypes
import numpy as np
import jax
import jax.numpy as jnp
from jax import lax
from jax.experimental import pallas as pl
from jax.experimental.pallas import tpu as pltpu

F32 = jnp.float32
BF16 = jnp.bfloat16

EPS = 1e-6
NEG = -1e30
BIG = 1e4
LANE = 128
VMEM_LIMIT = 56 * 1024 * 1024

PLE_DIM = 256
SB_HEADS = 16
NSA_HEADS = 16
NSA_GROUPS = 4
NSA_HG = NSA_HEADS // NSA_GROUPS
NSA_CMP_LEN = 32
NSA_CMP_STRIDE = 16
NSA_SEL_LEN = 64
NSA_SEL_TOPK = 16
NSA_WINDOW = 512
NSA_BRANCHES = 3
GM_GROUPS = 16
GM_CHUNK = 128

LOG2E = np.float32(np.log2(np.e))
SB_UNDERFLOW = 110.0
NSA_MASK = 2.0 ** 100
NSA_IND_LANE = LANE // 2


def _dot(a, b):
    return jnp.dot(a, b, preferred_element_type=F32)


def _dot_nt(a, b):
    return lax.dot_general(a, b, (((1,), (1,)), ((), ())), preferred_element_type=F32)


def _gelu(x):
    return 0.5 * x * (1.0 + lax.erf(x * np.float32(1.0 / np.sqrt(2.0))))


def _silu(x):
    return x * jax.nn.sigmoid(x)


def _norm_proj_kernel(x_ref, g_ref, w_ref, cs_ref, *rest, nb1, n_out):
    hn_ref = rest[-1]
    outs = rest[:n_out]
    j = pl.program_id(1)

    @pl.when(j == 0)
    def _():
        xf = x_ref[...]
        ms = jnp.mean(xf * xf, axis=-1, keepdims=True)
        hn_ref[...] = (xf * lax.rsqrt(ms + EPS) * g_ref[...]).astype(BF16)

    def compute():
        return _dot(hn_ref[...], w_ref[0]) * cs_ref[...]

    if n_out == 1:
        outs[0][...] = compute().astype(outs[0].dtype)
    else:
        @pl.when(j < nb1)
        def _():
            outs[0][...] = compute().astype(outs[0].dtype)

        @pl.when(j >= nb1)
        def _():
            outs[1][...] = compute().astype(outs[1].dtype)


def _norm_proj(x2d, g, w, layer, col_scale, n1, *, tm=1024, tn=512):
    n, d = x2d.shape
    ntot = w.shape[2]
    n2 = ntot - n1
    tm = min(tm, n)
    assert n % tm == 0 and n1 % tn == 0 and n2 % tn == 0
    nb1 = n1 // tn
    out_shape, out_specs = [], []
    if n1:
        out_shape.append(jax.ShapeDtypeStruct((n, n1), BF16))
        out_specs.append(pl.BlockSpec((tm, tn), lambda i, j: (i, jnp.minimum(j, nb1 - 1))))
    if n2:
        out_shape.append(jax.ShapeDtypeStruct((n, n2), F32))
        out_specs.append(pl.BlockSpec((tm, tn), lambda i, j: (i, jnp.maximum(j - nb1, 0))))
    outs = pl.pallas_call(
        functools.partial(_norm_proj_kernel, nb1=nb1, n_out=len(out_shape)),
        grid=(n // tm, ntot // tn),
        in_specs=[
            pl.BlockSpec((tm, d), lambda i, j: (i, 0)),
            pl.BlockSpec((1, d), lambda i, j: (0, 0)),
            pl.BlockSpec((1, d, tn), lambda i, j: (layer, 0, j)),
            pl.BlockSpec((1, tn), lambda i, j: (0, j)),
        ],
        out_specs=out_specs,
        out_shape=out_shape,
        scratch_shapes=[pltpu.VMEM((tm, d), BF16)],
        compiler_params=pltpu.CompilerParams(
            dimension_semantics=("arbitrary", "arbitrary"), vmem_limit_bytes=VMEM_LIMIT),
        name="norm_proj",
    )(x2d, g.reshape(1, d), w, col_scale.reshape(1, ntot))
    outs = list(outs)
    o1 = outs.pop(0) if n1 else None
    o2 = outs.pop(0) if n2 else None
    return o1, o2


def _out_ple_kernel(y_ref, x_ref, p_ref, wo_ref, wg_ref, wp_ref, fg_ref, o_ref, *, final):
    x2 = x_ref[...] + _dot(y_ref[...], wo_ref[0])
    gate = jax.nn.sigmoid(_dot(x2.astype(BF16), wg_ref[0]))
    proj = _dot(p_ref[0].astype(BF16), wp_ref[0])
    x3 = x2 + gate * proj
    if final:
        ms = jnp.mean(x3 * x3, axis=-1, keepdims=True)
        x3 = x3 * lax.rsqrt(ms + EPS) * fg_ref[...]
    o_ref[...] = x3


def _out_ple(y, x2d, p3d, layer, wo, wo_layer, wg, wp, fg, *, final, tm=512):
    n, d = x2d.shape
    pd = p3d.shape[2]
    tm = min(tm, n)
    assert n % tm == 0
    slab = lambda shape, l: pl.BlockSpec((1,) + shape, lambda i: (l, 0, 0), pipeline_mode=pl.Buffered(1))
    return pl.pallas_call(
        functools.partial(_out_ple_kernel, final=final),
        grid=(n // tm,),
        in_specs=[
            pl.BlockSpec((tm, d), lambda i: (i, 0)),
            pl.BlockSpec((tm, d), lambda i: (i, 0)),
            pl.BlockSpec((1, tm, pd), lambda i: (layer, i, 0)),
            slab((d, d), wo_layer), slab((d, d), layer), slab((pd, d), layer),
            pl.BlockSpec((1, d), lambda i: (0, 0), pipeline_mode=pl.Buffered(1)),
        ],
        out_specs=pl.BlockSpec((tm, d), lambda i: (i, 0)),
        out_shape=jax.ShapeDtypeStruct((n, d), F32),
        compiler_params=pltpu.CompilerParams(
            dimension_semantics=("arbitrary",), vmem_limit_bytes=VMEM_LIMIT),
        name="out_ple",
    )(y, x2d, p3d, wo, wg, wp, fg.reshape(1, d))


def _sb_kernel(q_ref, k_ref, v_ref, z_ref, o_ref, carry_ref, acc_ref, *, tq, hp):
    i = pl.program_id(2)
    rows = lax.broadcasted_iota(jnp.int32, (tq, tq), 0)
    cols = lax.broadcasted_iota(jnp.int32, (tq, tq), 1)
    later = jnp.where(rows > cols, 1.0, 0.0).astype(BF16)
    later2 = jnp.concatenate([later, later], axis=0)
    strictly_causal = cols < rows

    def tile(kt, diag):
        start = pl.multiple_of(kt * tq, tq)
        heads = [slice(hh * LANE, (hh + 1) * LANE) for hh in range(hp)]
        zs = [_dot_nt(q_ref[0, :, cs], k_ref[0, pl.ds(start, tq), cs]) for cs in heads]
        l1ps = [jnp.log2(1.0 + jnp.exp2(-jnp.abs(z))) for z in zs]
        lms, tails = [], []
        for z, l1p in zip(zs, l1ps):
            sp = jnp.maximum(z, 0.0) + l1p
            lm = jnp.where(strictly_causal, sp, 0.0) if diag else sp
            hi = lm.astype(BF16)
            lo = (lm - hi.astype(F32)).astype(BF16)
            lms.append(lm)
            tails.append(_dot(jnp.concatenate([hi, lo], axis=1), later2))
        for hh, (z, l1p, lm, tail) in enumerate(zip(zs, l1ps, lms, tails)):
            lsz = jnp.minimum(z, 0.0) - l1p
            carry = jnp.concatenate([carry_ref[hh]] * (tq // LANE), axis=1)
            a = jnp.exp2(lsz - tail - carry)
            if diag:
                a = jnp.where(strictly_causal, a, 0.0)
            acc_ref[hh] += _dot(a.astype(BF16), v_ref[0, pl.ds(start, tq), heads[hh]])
            carry_ref[hh] += jnp.sum(lm, axis=1, keepdims=True)

    carry_ref[...] = jnp.zeros_like(carry_ref)
    acc_ref[...] = jnp.zeros_like(acc_ref)
    tile(i, True)

    def cond(st):
        kt, cmin = st
        return jnp.logical_and(kt >= 0, cmin < SB_UNDERFLOW * LOG2E)

    def body(st):
        kt, _ = st
        tile(kt, False)
        return kt - 1, jnp.min(carry_ref[...])

    lax.while_loop(cond, body, (i - 1, jnp.min(carry_ref[...])))
    for hh in range(hp):
        cs = slice(hh * LANE, (hh + 1) * LANE)
        o_ref[0, :, cs] = (acc_ref[hh] * _silu(z_ref[0, :, cs])).astype(o_ref.dtype)


def _sb_attention(qkv, zf, b, s, *, tq=256, hp=4):
    h = SB_HEADS
    tq = min(tq, s)
    assert s % tq == 0 and h % hp == 0
    hb = h // hp
    w = hp * LANE
    return pl.pallas_call(
        functools.partial(_sb_kernel, tq=tq, hp=hp),
        grid=(b, hb, s // tq),
        in_specs=[
            pl.BlockSpec((1, tq, w), lambda bi, hi, i: (bi, i, hi)),
            pl.BlockSpec((1, s, w), lambda bi, hi, i: (bi, 0, hb + hi)),
            pl.BlockSpec((1, s, w), lambda bi, hi, i: (bi, 0, 2 * hb + hi)),
            pl.BlockSpec((1, tq, w), lambda bi, hi, i: (bi, i, hi)),
        ],
        out_specs=pl.BlockSpec((1, tq, w), lambda bi, hi, i: (bi, i, hi)),
        out_shape=jax.ShapeDtypeStruct((b, s, h * LANE), BF16),
        scratch_shapes=[pltpu.VMEM((hp, tq, LANE), F32), pltpu.VMEM((hp, tq, LANE), F32)],
        compiler_params=pltpu.CompilerParams(
            dimension_semantics=("arbitrary", "arbitrary", "arbitrary"),
            vmem_limit_bytes=VMEM_LIMIT),
        name="sb_attention",
    )(qkv, qkv, qkv, zf)


def _cmp_kernel(x_ref, pos_ref, w1_ref, w2_ref, o_ref, *, ncb):
    half = NSA_CMP_LEN // 2
    y1 = jnp.zeros((ncb, LANE), F32)
    y2 = jnp.zeros((ncb, LANE), F32)
    for l in range(NSA_CMP_LEN):
        xl = x_ref[0, pl.ds(l % half, ncb, stride=NSA_CMP_STRIDE), :]
        xl = (xl + pos_ref[0, l:l + 1, :]).astype(BF16)
        part = _dot(xl, w1_ref[0, l * LANE:(l + 1) * LANE, :])
        if l < half:
            y1 = y1 + part
        else:
            y2 = y2 + part
    pre = y1 + pltpu.roll(y2, ncb - 1, 0)
    out = _dot(_gelu(pre).astype(BF16), w2_ref[0])
    valid = lax.broadcasted_iota(jnp.int32, (ncb, LANE), 0) < ncb - 1
    o_ref[0, 0, 0] = jnp.where(valid, out, 0.0).astype(o_ref.dtype)


def _nsa_compress(pf, pos, w1, w2, b, s, col0):
    g = NSA_GROUPS
    ncb = s // NSA_CMP_STRIDE
    return pl.pallas_call(
        functools.partial(_cmp_kernel, ncb=ncb),
        grid=(2, b, g),
        in_specs=[
            pl.BlockSpec((1, s, LANE), lambda kv, bi, gi: (bi, 0, col0 + kv * g + gi)),
            pl.BlockSpec((1, NSA_CMP_LEN, LANE), lambda kv, bi, gi: (kv, 0, 0)),
            pl.BlockSpec((1, NSA_CMP_LEN * LANE, LANE), lambda kv, bi, gi: (kv, 0, 0)),
            pl.BlockSpec((1, LANE, LANE), lambda kv, bi, gi: (kv, 0, 0)),
        ],
        out_specs=pl.BlockSpec((1, 1, 1, ncb, LANE), lambda kv, bi, gi: (kv, bi, gi, 0, 0)),
        out_shape=jax.ShapeDtypeStruct((2, b, g, ncb, LANE), BF16),
        compiler_params=pltpu.CompilerParams(
            dimension_semantics=("arbitrary", "arbitrary", "arbitrary"),
            vmem_limit_bytes=VMEM_LIMIT),
        name="nsa_compress",
    )(pf, pos, w1, w2)


def _split3(x):
    x = np.asarray(x, np.float32)
    bf = ml_dtypes.bfloat16
    hi = x.astype(bf).astype(np.float32)
    mid = (x - hi).astype(bf).astype(np.float32)
    lo = (x - hi - mid).astype(bf).astype(np.float32)
    return hi, mid, lo


def _nsa_constants(s, tq, tk):
    h, g, hg = NSA_HEADS, NSA_GROUPS, NSA_HG
    bf = ml_dtypes.bfloat16
    slopes = np.exp2(-8.0 * np.arange(1, h + 1) / h).astype(np.float32) * LOG2E
    qbase = np.zeros((g, hg * tq, LANE), np.float32)
    for gi in range(g):
        for n in range(hg):
            pieces = np.stack(_split3(slopes[gi * hg + n]))
            qbase[gi, n * tq:(n + 1) * tq, 0:3] = pieces
            qbase[gi, n * tq:(n + 1) * tq, 3:6] = pieces
    pos = np.arange(s)
    kaug = np.zeros((s, LANE), np.float32)
    kaug[:, 0:3] = (pos % tk)[:, None]
    kaug[:, 3:6] = (pos - pos % tk)[:, None]
    kaug[pos, NSA_IND_LANE + (pos // NSA_SEL_LEN) % NSA_IND_LANE] = 1.0
    ratio = NSA_SEL_LEN // NSA_CMP_STRIDE
    nsel = s // NSA_SEL_LEN
    cc = np.arange(s // NSA_CMP_STRIDE)
    cend = (ratio * (cc % nsel) + cc // nsel) * NSA_CMP_STRIDE + (NSA_CMP_LEN - 1)
    caug = np.zeros((cc.size, LANE), np.float32)
    caug[:, 0:3] = (cend % tk)[:, None]
    caug[:, 3:6] = (cend - cend % tk)[:, None]
    return (jnp.asarray(qbase.astype(bf)), jnp.asarray(kaug.astype(bf)), jnp.asarray(caug.astype(bf)))


def _nsa_kernel(q_ref, ks_ref, vs_ref, kw_ref, vw_ref, kaug_ref, kc_ref, vc_ref, caug_ref, qbase_ref,
                gate_ref, z_ref, y_ref, qp_ref, m_ref, l_ref, acc_ref, sc_ref, flag_ref, tiles_ref,
                *, tq, tk, nsel):
    i = pl.program_id(2)
    hg = NSA_HG
    r = hg * tq
    ratio = NSA_SEL_LEN // NSA_CMP_STRIDE
    ncb = nsel * ratio
    bpt = tk // NSA_SEL_LEN
    n_tiles = nsel // bpt
    halves_per_tile = NSA_IND_LANE // bpt
    t0 = i * tq

    qg = q_ref[0]
    qall = jnp.concatenate([qg[:, n * LANE:(n + 1) * LANE] for n in range(hg)], axis=0)
    base = qbase_ref[0]
    qwin = jnp.concatenate([qall, base], axis=1)
    trow = t0 + lax.broadcasted_iota(jnp.int32, (r, 1), 0) % tq
    last = (t0 + tq - 1) // tk
    dist0 = (lax.broadcasted_iota(jnp.int32, (r, tk), 0) % tq
             - lax.broadcasted_iota(jnp.int32, (r, tk), 1))

    def window_branch():
        n_win = NSA_WINDOW // tk + 1
        win_sc, win_v = [], []
        for j in range(n_win):
            kt = last - (n_win - 1) + j
            start = pl.multiple_of(jnp.maximum(kt, 0) * tk, tk)
            kp = jnp.concatenate([kw_ref[0, pl.ds(start, tk), :], kaug_ref[pl.ds(start, tk), :]], axis=1)
            off = kt * tk - t0
            if j == n_win - 1:
                mask = dist0 >= off
            elif j == 0:
                mask = dist0 < jnp.where(kt >= 0, off + NSA_WINDOW, -tk)
            else:
                mask = kt >= 0
            sc = jnp.where(mask, _dot_nt(qwin, kp), NEG)
            win_sc += [sc[:, c * LANE:(c + 1) * LANE] for c in range(tk // LANE)]
            win_v.append(vw_ref[0, pl.ds(start, tk), :])
        m_w = win_sc[0]
        for ch in win_sc[1:]:
            m_w = jnp.maximum(m_w, ch)
        m_w = jnp.max(m_w, axis=1, keepdims=True)
        win_p = [jnp.exp2(ch - m_w) for ch in win_sc]
        l_w = win_p[0]
        for x in win_p[1:]:
            l_w = l_w + x
        l_w = jnp.sum(l_w, axis=1, keepdims=True)
        cpt = tk // LANE
        o_w = _dot(jnp.concatenate([x.astype(BF16) for x in win_p[:cpt]], axis=1), win_v[0])
        for j in range(1, n_win):
            o_w = o_w + _dot(jnp.concatenate([x.astype(BF16) for x in win_p[j * cpt:(j + 1) * cpt]], axis=1),
                             win_v[j])
        return o_w / l_w

    cc = lax.broadcasted_iota(jnp.int32, (1, ncb), 1)
    cend = (ratio * (cc % nsel) + cc // nsel) * NSA_CMP_STRIDE + (NSA_CMP_LEN - 1)
    mask_c = trow >= cend
    kcp = jnp.concatenate([kc_ref[0, 0], caug_ref[...]], axis=1)
    s_c = jnp.where(mask_c, _dot_nt(qwin, kcp), NEG)
    m_c = jnp.max(s_c, axis=1, keepdims=True)
    e_c = jnp.exp2(s_c - m_c)
    l_c = jnp.sum(e_c, axis=1, keepdims=True)
    p_c = e_c * jnp.where(trow >= NSA_CMP_LEN - 1, 1.0 / l_c, 0.0)
    o_c = _dot(p_c.astype(BF16), vc_ref[0, 0])

    ph = p_c[0:tq]
    for n in range(1, hg):
        ph = ph + p_c[n * tq:(n + 1) * tq]
    imp = ph[:, 0:nsel]
    for a in range(1, ratio):
        imp = imp + ph[:, a * nsel:(a + 1) * nsel]
    if nsel < LANE:
        imp = jnp.concatenate([imp, jnp.zeros((tq, LANE - nsel), F32)], axis=1)

    imp_t = imp.T
    jj = lax.broadcasted_iota(jnp.int32, (LANE, tq), 0)
    cur = (t0 + lax.broadcasted_iota(jnp.int32, (1, tq), 1)) // NSA_SEL_LEN
    forced = (jj == 0) | (jj == cur) | (jj == cur - 1)
    val = jnp.where(forced, BIG, jnp.where(jj <= cur, imp_t, -BIG))
    taken = -(2.0 ** 127)
    for _ in range(min(NSA_SEL_TOPK, nsel)):
        vmax = jnp.max(val, axis=0, keepdims=True)
        first = jnp.min(jnp.where(val == vmax, jj, LANE), axis=0, keepdims=True)
        val = jnp.where(jj == first, taken, val)
    sel = jnp.where(val == taken, 1.0, 0.0).T

    o_w = window_branch()

    col_any = jnp.max(sel, axis=0, keepdims=True)
    for kt in range(n_tiles):
        flag_ref[kt] = (jnp.max(col_any[:, kt * bpt:(kt + 1) * bpt]) > 0.5).astype(jnp.int32)

    lane_q = lax.broadcasted_iota(jnp.int32, (tq, LANE), 1)
    lane_r = lax.broadcasted_iota(jnp.int32, (r, LANE), 1)
    unpicked = jnp.where(sel > 0.5, 0.0, -NSA_MASK)
    for half, part in enumerate((pltpu.roll(unpicked, NSA_IND_LANE, 1), unpicked)):
        part = jnp.where(lane_q >= NSA_IND_LANE, part, 0.0).astype(BF16)
        aug = jnp.where(lane_r >= NSA_IND_LANE, jnp.concatenate([part] * hg, axis=0), base)
        qp_ref[half] = jnp.concatenate([qall, aug], axis=1)

    m_ref[...] = jnp.full_like(m_ref, NEG)
    l_ref[...] = jnp.zeros_like(l_ref)
    acc_ref[...] = jnp.zeros_like(acc_ref)

    def compact(kt, n):
        tiles_ref[n] = kt
        return n + flag_ref[kt]

    n_act = lax.fori_loop(0, last, compact, 0)
    tiles_ref[n_act] = last

    def scores(kt):
        start = pl.multiple_of(kt * tk, tk)
        kp = jnp.concatenate([ks_ref[0, pl.ds(start, tk), :], kaug_ref[pl.ds(start, tk), :]], axis=1)
        return _dot_nt(qp_ref[kt // halves_per_tile], kp)

    def softmax_step(sc, kt):
        start = pl.multiple_of(kt * tk, tk)
        chunks = [sc[:, c * LANE:(c + 1) * LANE] for c in range(tk // LANE)]
        cmax = chunks[0]
        for ch in chunks[1:]:
            cmax = jnp.maximum(cmax, ch)
        m_old = m_ref[...]
        m_new = jnp.maximum(m_old, jnp.max(cmax, axis=1, keepdims=True))
        alpha = jnp.exp2(m_old - m_new)
        ps = [jnp.exp2(ch - m_new) for ch in chunks]
        psum = ps[0]
        for x in ps[1:]:
            psum = psum + x
        l_ref[...] = alpha * l_ref[...] + jnp.sum(psum, axis=1, keepdims=True)
        p = jnp.concatenate([x.astype(BF16) for x in ps], axis=1)
        acc_ref[...] = alpha * acc_ref[...] + _dot(p, vs_ref[0, pl.ds(start, tk), :])
        m_ref[...] = m_new

    sc_ref[0] = scores(tiles_ref[0])

    def step(j, slot):
        sc_ref[1 - slot] = scores(tiles_ref[j + 1])
        softmax_step(sc_ref[slot], tiles_ref[j])

    def pair_body(pj, c):
        step(2 * pj, 0)
        step(2 * pj + 1, 1)
        return c

    lax.fori_loop(0, n_act // 2, pair_body, 0)

    @pl.when(n_act % 2 == 1)
    def _():
        step(n_act - 1, 0)

    softmax_step(jnp.where(dist0 >= last * tk - t0, sc_ref[n_act % 2], NEG), last)
    o_s = acc_ref[...] / l_ref[...]

    gates = jax.nn.sigmoid(gate_ref[0])
    zg = z_ref[0]
    for n in range(hg):
        rs = slice(n * tq, (n + 1) * tq)
        o = (gates[:, n:n + 1] * o_c[rs]
             + gates[:, hg + n:hg + n + 1] * o_s[rs]
             + gates[:, 2 * hg + n:2 * hg + n + 1] * o_w[rs])
        cs = slice(n * LANE, (n + 1) * LANE)
        y_ref[0, :, cs] = (o * _silu(zg[:, cs])).astype(y_ref.dtype)


def _nsa_attention(pb, pf, kvc, b, s, *, tq=128, tk=256):
    g, hg = NSA_GROUPS, NSA_HG
    tq = min(tq, s)
    tk = min(tk, s)
    nsel = s // NSA_SEL_LEN
    ncb = s // NSA_CMP_STRIDE
    r = hg * tq
    assert s % tk == 0 and tk % tq == 0 and tk % LANE == 0 and nsel <= LANE and tk <= NSA_WINDOW
    assert NSA_IND_LANE % (tk // NSA_SEL_LEN) == 0 and tk <= 256
    qbase, kaug, caug = _nsa_constants(s, tq, tk)
    qblocks = NSA_HEADS
    zblocks = NSA_HEADS
    kv_spec = lambda off: pl.BlockSpec((1, s, LANE), lambda bi, gi, i: (bi, 0, qblocks + off * g + gi))
    return pl.pallas_call(
        functools.partial(_nsa_kernel, tq=tq, tk=tk, nsel=nsel),
        grid=(b, g, s // tq),
        in_specs=[
            pl.BlockSpec((1, tq, hg * LANE), lambda bi, gi, i: (bi, i, gi)),
            kv_spec(0), kv_spec(1), kv_spec(2), kv_spec(3),
            pl.BlockSpec((s, LANE), lambda bi, gi, i: (0, 0)),
            pl.BlockSpec((1, 1, ncb, LANE), lambda bi, gi, i: (bi, gi, 0, 0)),
            pl.BlockSpec((1, 1, ncb, LANE), lambda bi, gi, i: (bi, gi, 0, 0)),
            pl.BlockSpec((ncb, LANE), lambda bi, gi, i: (0, 0)),
            pl.BlockSpec((1, r, LANE), lambda bi, gi, i: (gi, 0, 0)),
            pl.BlockSpec((1, tq, LANE), lambda bi, gi, i: (bi, i, zblocks + gi)),
            pl.BlockSpec((1, tq, hg * LANE), lambda bi, gi, i: (bi, i, gi)),
        ],
        out_specs=pl.BlockSpec((1, tq, hg * LANE), lambda bi, gi, i: (bi, i, gi)),
        out_shape=jax.ShapeDtypeStruct((b, s, NSA_HEADS * LANE), BF16),
        scratch_shapes=[
            pltpu.VMEM((2, r, 2 * LANE), BF16),
            pltpu.VMEM((r, LANE), F32), pltpu.VMEM((r, LANE), F32), pltpu.VMEM((r, LANE), F32),
            pltpu.VMEM((2, r, tk), F32),
            pltpu.SMEM((nsel // (tk // NSA_SEL_LEN),), jnp.int32),
            pltpu.SMEM((nsel // (tk // NSA_SEL_LEN) + 1,), jnp.int32),
        ],
        compiler_params=pltpu.CompilerParams(
            dimension_semantics=("arbitrary", "arbitrary", "arbitrary"),
            vmem_limit_bytes=VMEM_LIMIT),
        name="nsa_attention",
    )(pb, pb, pb, pb, pb, kaug, kvc[0], kvc[1], caug, qbase, pf, pf)


def _gmlp_kernel(u_ref, v_ref, z_ref, vg_ref, ws_ref, bst_ref, y_ref, *, n_chunks):
    c = GM_CHUNK
    v = _gelu(v_ref[...])
    ms = jnp.mean(v * v, axis=-1, keepdims=True)
    vn = (v * lax.rsqrt(ms + EPS) * vg_ref[...]).astype(BF16)
    rows = lax.broadcasted_iota(jnp.int32, (c, c), 0)
    cols = lax.broadcasted_iota(jnp.int32, (c, c), 1)
    causal = cols <= rows
    bst = bst_ref[...]
    for gi in range(GM_GROUPS):
        ws = jnp.where(causal, ws_ref[gi], 0.0).astype(BF16)
        cs = slice(gi * LANE, (gi + 1) * LANE)
        for ci in range(n_chunks):
            rs = slice(ci * c, (ci + 1) * c)
            mixed = _dot(ws, vn[rs, cs]) + bst[:, gi:gi + 1]
            y = _gelu(u_ref[rs, cs]) * mixed * _silu(z_ref[rs, cs])
            y_ref[rs, cs] = y.astype(y_ref.dtype)


def _gmlp(pf, vg, ws, bs, *, n_chunks=2):
    n = pf.shape[0]
    w = pf.shape[1] // 3
    tm = GM_CHUNK * n_chunks
    assert n % tm == 0 and w == GM_GROUPS * LANE
    return pl.pallas_call(
        functools.partial(_gmlp_kernel, n_chunks=n_chunks),
        grid=(n // tm,),
        in_specs=[
            pl.BlockSpec((tm, w), lambda i: (i, 0)),
            pl.BlockSpec((tm, w), lambda i: (i, 1)),
            pl.BlockSpec((tm, w), lambda i: (i, 2)),
            pl.BlockSpec((1, w), lambda i: (0, 0)),
            pl.BlockSpec((GM_GROUPS, GM_CHUNK, GM_CHUNK), lambda i: (0, 0, 0)),
            pl.BlockSpec((GM_CHUNK, GM_GROUPS), lambda i: (0, 0)),
        ],
        out_specs=pl.BlockSpec((tm, w), lambda i: (i, 0)),
        out_shape=jax.ShapeDtypeStruct((n, w), BF16),
        compiler_params=pltpu.CompilerParams(
            dimension_semantics=("arbitrary",), vmem_limit_bytes=VMEM_LIMIT),
        name="gmlp_mix",
    )(pf, pf, pf, vg.reshape(1, w), ws, bs.T)


def _sb_layer(x2d, b, s, norm_g, w_in, j):
    width = w_in.shape[2] // 4
    scale = np.float32(LANE ** -0.5) * LOG2E
    col_scale = jnp.concatenate([jnp.full((width,), scale, F32), jnp.ones((3 * width,), F32)])
    qkv, zf = _norm_proj(x2d, norm_g, w_in.astype(BF16), j, col_scale, 3 * width, tn=1024)
    y = _sb_attention(qkv.reshape(b, s, 3 * width), zf.reshape(b, s, width), b, s)
    return y.reshape(b * s, width)


def _nsa_layer(x2d, b, s, norm_g, w_in, pos_k, pos_v, ck_w1, ck_w2, cv_w1, cv_w2):
    d = x2d.shape[1]
    h, g, hg = NSA_HEADS, NSA_GROUPS, NSA_HG
    kvw = g * LANE
    qw = h * LANE
    o_kc, o_vc, o_ks, o_vs, o_kw, o_vw = (qw + a * kvw for a in range(6))
    o_g = qw + 6 * kvw
    o_z = o_g + NSA_BRANCHES * h
    wg = w_in[:, o_g:o_z].reshape(d, NSA_BRANCHES, g, hg).transpose(0, 2, 1, 3)
    wg = wg.reshape(d, g, NSA_BRANCHES * hg)
    wg = jnp.pad(wg, ((0, 0), (0, 0), (0, LANE - NSA_BRANCHES * hg))).reshape(d, g * LANE)
    w_all = jnp.concatenate([
        w_in[:, :qw], w_in[:, o_ks:o_g],
        w_in[:, o_z:], wg, w_in[:, o_kc:o_ks],
    ], axis=1).astype(BF16)
    n1 = qw + 4 * kvw
    scale = np.float32(LANE ** -0.5) * LOG2E
    col_scale = jnp.concatenate([jnp.full((qw,), scale, F32), jnp.ones((w_all.shape[1] - qw,), F32)])
    pb, pf = _norm_proj(x2d, norm_g, w_all[None], 0, col_scale, n1)
    pf = pf.reshape(b, s, pf.shape[1])
    pb = pb.reshape(b, s, n1)

    pos = jnp.stack([pos_k, pos_v])
    w1 = jnp.stack([ck_w1, cv_w1]).astype(BF16)
    w2 = jnp.stack([ck_w2, cv_w2]).astype(BF16)
    kvc = _nsa_compress(pf, pos, w1, w2, b, s, (qw + g * LANE) // LANE)
    ratio = NSA_SEL_LEN // NSA_CMP_STRIDE
    nsel = s // NSA_SEL_LEN
    kvc = kvc.reshape(2, b, g, nsel, ratio, LANE).transpose(0, 1, 2, 4, 3, 5).reshape(2, b, g, nsel * ratio, LANE)

    y = _nsa_attention(pb, pf, kvc, b, s)
    return y.reshape(b * s, qw)


def _gmlp_layer(x2d, norm_g, w_in, j, v_norm_g, w_s, b_s):
    ntot = w_in.shape[2]
    _, pf = _norm_proj(x2d, norm_g, w_in.astype(BF16), j, jnp.ones((ntot,), F32), 0, tn=1024)
    return _gmlp(pf, v_norm_g, w_s, b_s)


def kernel(x, p, norm_g, final_norm_g, ple_proj, ple_gate, sb_w_in, sb_w_out, nsa_w_in, nsa_cmp_pos_k, nsa_cmp_pos_v, nsa_cmp_k_w1, nsa_cmp_k_w2, nsa_cmp_v_w1, nsa_cmp_v_w2, nsa_w_out, gm_w_in, gm_v_norm_g, gm_w_s, gm_b_s, gm_w_out):
    b, s, d = x.shape
    depth = p.shape[0]
    x2d = x.reshape(b * s, d)
    for i in range(depth):
        kind, j = i % 3, i // 3
        if kind == 0:
            y, wo = _sb_layer(x2d, b, s, norm_g[i], sb_w_in, j), sb_w_out
        elif kind == 1:
            y, wo = _nsa_layer(x2d, b, s, norm_g[i], nsa_w_in[j], nsa_cmp_pos_k[j], nsa_cmp_pos_v[j],
                               nsa_cmp_k_w1[j], nsa_cmp_k_w2[j], nsa_cmp_v_w1[j], nsa_cmp_v_w2[j]), nsa_w_out
        else:
            y, wo = _gmlp_layer(x2d, norm_g[i], gm_w_in, j, gm_v_norm_g[j], gm_w_s[j], gm_b_s[j]), gm_w_out
        x2d = _out_ple(y, x2d, p.reshape(depth, b * s, p.shape[-1]), i, wo.astype(BF16), j,
                       ple_gate.astype(BF16), ple_proj.astype(BF16), final_norm_g,
                       final=(i == depth - 1))
    return x2d.reshape(b, s, d)
```

```python
import functools

import ml_dtypes
import numpy as np
import jax
import jax.numpy as jnp
from jax import lax
from jax.experimental import pallas as pl
from jax.experimental.pallas import tpu as pltpu

F32 = jnp.float32
BF16 = jnp.bfloat16

EPS = 1e-6
NEG = -1e30
BIG = 1e4
LANE = 128
VMEM_LIMIT = 56 * 1024 * 1024

PLE_DIM = 256
SB_HEADS = 16
NSA_HEADS = 16
NSA_GROUPS = 4
NSA_HG = NSA_HEADS // NSA_GROUPS
NSA_CMP_LEN = 32
NSA_CMP_STRIDE = 16
NSA_SEL_LEN = 64
NSA_SEL_TOPK = 16
NSA_WINDOW = 512
NSA_BRANCHES = 3
GM_GROUPS = 16
GM_CHUNK = 128

LOG2E = np.float32(np.log2(np.e))
SB_UNDERFLOW = 110.0
NSA_MASK = 2.0 ** 100
NSA_IND_LANE = LANE // 2


def _dot(a, b):
    return jnp.dot(a, b, preferred_element_type=F32)


def _dot_nt(a, b):
    return lax.dot_general(a, b, (((1,), (1,)), ((), ())), preferred_element_type=F32)


def _gelu(x):
    return 0.5 * x * (1.0 + lax.erf(x * np.float32(1.0 / np.sqrt(2.0))))


def _silu(x):
    return x * jax.nn.sigmoid(x)


def _norm_proj_kernel(x_ref, g_ref, w_ref, cs_ref, *rest, nb1, n_out):
    hn_ref = rest[-1]
    outs = rest[:n_out]
    j = pl.program_id(1)

    @pl.when(j == 0)
    def _():
        xf = x_ref[...]
        ms = jnp.mean(xf * xf, axis=-1, keepdims=True)
        hn_ref[...] = (xf * lax.rsqrt(ms + EPS) * g_ref[...]).astype(BF16)

    def compute():
        return _dot(hn_ref[...], w_ref[0]) * cs_ref[...]

    if n_out == 1:
        outs[0][...] = compute().astype(outs[0].dtype)
    else:
        @pl.when(j < nb1)
        def _():
            outs[0][...] = compute().astype(outs[0].dtype)

        @pl.when(j >= nb1)
        def _():
            outs[1][...] = compute().astype(outs[1].dtype)


def _norm_proj(x2d, g, w, layer, col_scale, n1, *, tm=1024, tn=512):
    n, d = x2d.shape
    ntot = w.shape[2]
    n2 = ntot - n1
    tm = min(tm, n)
    assert n % tm == 0 and n1 % tn == 0 and n2 % tn == 0
    nb1 = n1 // tn
    out_shape, out_specs = [], []
    if n1:
        out_shape.append(jax.ShapeDtypeStruct((n, n1), BF16))
        out_specs.append(pl.BlockSpec((tm, tn), lambda i, j: (i, jnp.minimum(j, nb1 - 1))))
    if n2:
        out_shape.append(jax.ShapeDtypeStruct((n, n2), F32))
        out_specs.append(pl.BlockSpec((tm, tn), lambda i, j: (i, jnp.maximum(j - nb1, 0))))
    outs = pl.pallas_call(
        functools.partial(_norm_proj_kernel, nb1=nb1, n_out=len(out_shape)),
        grid=(n // tm, ntot // tn),
        in_specs=[
            pl.BlockSpec((tm, d), lambda i, j: (i, 0)),
            pl.BlockSpec((1, d), lambda i, j: (0, 0)),
            pl.BlockSpec((1, d, tn), lambda i, j: (layer, 0, j)),
            pl.BlockSpec((1, tn), lambda i, j: (0, j)),
        ],
        out_specs=out_specs,
        out_shape=out_shape,
        scratch_shapes=[pltpu.VMEM((tm, d), BF16)],
        compiler_params=pltpu.CompilerParams(
            dimension_semantics=("arbitrary", "arbitrary"), vmem_limit_bytes=VMEM_LIMIT),
        name="norm_proj",
    )(x2d, g.reshape(1, d), w, col_scale.reshape(1, ntot))
    outs = list(outs)
    o1 = outs.pop(0) if n1 else None
    o2 = outs.pop(0) if n2 else None
    return o1, o2


def _out_ple_kernel(y_ref, x_ref, p_ref, wo_ref, wg_ref, wp_ref, fg_ref, o_ref, *, final):
    x2 = x_ref[...] + _dot(y_ref[...], wo_ref[0])
    gate = jax.nn.sigmoid(_dot(x2.astype(BF16), wg_ref[0]))
    proj = _dot(p_ref[0].astype(BF16), wp_ref[0])
    x3 = x2 + gate * proj
    if final:
        ms = jnp.mean(x3 * x3, axis=-1, keepdims=True)
        x3 = x3 * lax.rsqrt(ms + EPS) * fg_ref[...]
    o_ref[...] = x3


def _out_ple(y, x2d, p3d, layer, wo, wo_layer, wg, wp, fg, *, final, tm=512):
    n, d = x2d.shape
    pd = p3d.shape[2]
    tm = min(tm, n)
    assert n % tm == 0
    slab = lambda shape, l: pl.BlockSpec((1,) + shape, lambda i: (l, 0, 0), pipeline_mode=pl.Buffered(1))
    return pl.pallas_call(
        functools.partial(_out_ple_kernel, final=final),
        grid=(n // tm,),
        in_specs=[
            pl.BlockSpec((tm, d), lambda i: (i, 0)),
            pl.BlockSpec((tm, d), lambda i: (i, 0)),
            pl.BlockSpec((1, tm, pd), lambda i: (layer, i, 0)),
            slab((d, d), wo_layer), slab((d, d), layer), slab((pd, d), layer),
            pl.BlockSpec((1, d), lambda i: (0, 0), pipeline_mode=pl.Buffered(1)),
        ],
        out_specs=pl.BlockSpec((tm, d), lambda i: (i, 0)),
        out_shape=jax.ShapeDtypeStruct((n, d), F32),
        compiler_params=pltpu.CompilerParams(
            dimension_semantics=("arbitrary",), vmem_limit_bytes=VMEM_LIMIT),
        name="out_ple",
    )(y, x2d, p3d, wo, wg, wp, fg.reshape(1, d))


def _sb_kernel(q_ref, k_ref, v_ref, z_ref, o_ref, carry_ref, acc_ref, *, tq, hp):
    i = pl.program_id(2)
    rows = lax.broadcasted_iota(jnp.int32, (tq, tq), 0)
    cols = lax.broadcasted_iota(jnp.int32, (tq, tq), 1)
    later = jnp.where(rows > cols, 1.0, 0.0).astype(BF16)
    later2 = jnp.concatenate([later, later], axis=0)
    strictly_causal = cols < rows

    def tile(kt, diag):
        start = pl.multiple_of(kt * tq, tq)
        heads = [slice(hh * LANE, (hh + 1) * LANE) for hh in range(hp)]
        zs = [_dot_nt(q_ref[0, :, cs], k_ref[0, pl.ds(start, tq), cs]) for cs in heads]
        l1ps = [jnp.log2(1.0 + jnp.exp2(-jnp.abs(z))) for z in zs]
        lms, tails = [], []
        for z, l1p in zip(zs, l1ps):
            sp = jnp.maximum(z, 0.0) + l1p
            lm = jnp.where(strictly_causal, sp, 0.0) if diag else sp
            hi = lm.astype(BF16)
            lo = (lm - hi.astype(F32)).astype(BF16)
            lms.append(lm)
            tails.append(_dot(jnp.concatenate([hi, lo], axis=1), later2))
        for hh, (z, l1p, lm, tail) in enumerate(zip(zs, l1ps, lms, tails)):
            lsz = jnp.minimum(z, 0.0) - l1p
            carry = jnp.concatenate([carry_ref[hh]] * (tq // LANE), axis=1)
            a = jnp.exp2(lsz - tail - carry)
            if diag:
                a = jnp.where(strictly_causal, a, 0.0)
            acc_ref[hh] += _dot(a.astype(BF16), v_ref[0, pl.ds(start, tq), heads[hh]])
            carry_ref[hh] += jnp.sum(lm, axis=1, keepdims=True)

    carry_ref[...] = jnp.zeros_like(carry_ref)
    acc_ref[...] = jnp.zeros_like(acc_ref)
    tile(i, True)

    def cond(st):
        kt, cmin = st
        return jnp.logical_and(kt >= 0, cmin < SB_UNDERFLOW * LOG2E)

    def body(st):
        kt, _ = st
        tile(kt, False)
        return kt - 1, jnp.min(carry_ref[...])

    lax.while_loop(cond, body, (i - 1, jnp.min(carry_ref[...])))
    for hh in range(hp):
        cs = slice(hh * LANE, (hh + 1) * LANE)
        o_ref[0, :, cs] = (acc_ref[hh] * _silu(z_ref[0, :, cs])).astype(o_ref.dtype)


def _sb_attention(qkv, zf, b, s, *, tq=256, hp=4):
    h = SB_HEADS
    tq = min(tq, s)
    assert s % tq == 0 and h % hp == 0
    hb = h // hp
    w = hp * LANE
    return pl.pallas_call(
        functools.partial(_sb_kernel, tq=tq, hp=hp),
        grid=(b, hb, s // tq),
        in_specs=[
            pl.BlockSpec((1, tq, w), lambda bi, hi, i: (bi, i, hi)),
            pl.BlockSpec((1, s, w), lambda bi, hi, i: (bi, 0, hb + hi)),
            pl.BlockSpec((1, s, w), lambda bi, hi, i: (bi, 0, 2 * hb + hi)),
            pl.BlockSpec((1, tq, w), lambda bi, hi, i: (bi, i, hi)),
        ],
        out_specs=pl.BlockSpec((1, tq, w), lambda bi, hi, i: (bi, i, hi)),
        out_shape=jax.ShapeDtypeStruct((b, s, h * LANE), BF16),
        scratch_shapes=[pltpu.VMEM((hp, tq, LANE), F32), pltpu.VMEM((hp, tq, LANE), F32)],
        compiler_params=pltpu.CompilerParams(
            dimension_semantics=("arbitrary", "arbitrary", "arbitrary"),
            vmem_limit_bytes=VMEM_LIMIT),
        name="sb_attention",
    )(qkv, qkv, qkv, zf)


def _cmp_kernel(x_ref, pos_ref, w1_ref, w2_ref, o_ref, *, ncb):
    half = NSA_CMP_LEN // 2
    y1 = jnp.zeros((ncb, LANE), F32)
    y2 = jnp.zeros((ncb, LANE), F32)
    for l in range(NSA_CMP_LEN):
        xl = x_ref[0, pl.ds(l % half, ncb, stride=NSA_CMP_STRIDE), :]
        xl = (xl + pos_ref[0, l:l + 1, :]).astype(BF16)
        part = _dot(xl, w1_ref[0, l * LANE:(l + 1) * LANE, :])
        if l < half:
            y1 = y1 + part
        else:
            y2 = y2 + part
    pre = y1 + pltpu.roll(y2, ncb - 1, 0)
    out = _dot(_gelu(pre).astype(BF16), w2_ref[0])
    valid = lax.broadcasted_iota(jnp.int32, (ncb, LANE), 0) < ncb - 1
    o_ref[0, 0, 0] = jnp.where(valid, out, 0.0).astype(o_ref.dtype)


def _nsa_compress(pf, pos, w1, w2, b, s, col0):
    g = NSA_GROUPS
    ncb = s // NSA_CMP_STRIDE
    return pl.pallas_call(
        functools.partial(_cmp_kernel, ncb=ncb),
        grid=(2, b, g),
        in_specs=[
            pl.BlockSpec((1, s, LANE), lambda kv, bi, gi: (bi, 0, col0 + kv * g + gi)),
            pl.BlockSpec((1, NSA_CMP_LEN, LANE), lambda kv, bi, gi: (kv, 0, 0)),
            pl.BlockSpec((1, NSA_CMP_LEN * LANE, LANE), lambda kv, bi, gi: (kv, 0, 0)),
            pl.BlockSpec((1, LANE, LANE), lambda kv, bi, gi: (kv, 0, 0)),
        ],
        out_specs=pl.BlockSpec((1, 1, 1, ncb, LANE), lambda kv, bi, gi: (kv, bi, gi, 0, 0)),
        out_shape=jax.ShapeDtypeStruct((2, b, g, ncb, LANE), BF16),
        compiler_params=pltpu.CompilerParams(
            dimension_semantics=("arbitrary", "arbitrary", "arbitrary"),
            vmem_limit_bytes=VMEM_LIMIT),
        name="nsa_compress",
    )(pf, pos, w1, w2)


def _split3(x):
    x = np.asarray(x, np.float32)
    bf = ml_dtypes.bfloat16
    hi = x.astype(bf).astype(np.float32)
    mid = (x - hi).astype(bf).astype(np.float32)
    lo = (x - hi - mid).astype(bf).astype(np.float32)
    return hi, mid, lo


def _nsa_constants(s, tq, tk):
    h, g, hg = NSA_HEADS, NSA_GROUPS, NSA_HG
    bf = ml_dtypes.bfloat16
    slopes = np.exp2(-8.0 * np.arange(1, h + 1) / h).astype(np.float32) * LOG2E
    qbase = np.zeros((g, hg * tq, LANE), np.float32)
    for gi in range(g):
        for n in range(hg):
            pieces = np.stack(_split3(slopes[gi * hg + n]))
            qbase[gi, n * tq:(n + 1) * tq, 0:3] = pieces
            qbase[gi, n * tq:(n + 1) * tq, 3:6] = pieces
    pos = np.arange(s)
    kaug = np.zeros((s, LANE), np.float32)
    kaug[:, 0:3] = (pos % tk)[:, None]
    kaug[:, 3:6] = (pos - pos % tk)[:, None]
    kaug[pos, NSA_IND_LANE + (pos // NSA_SEL_LEN) % NSA_IND_LANE] = 1.0
    ratio = NSA_SEL_LEN // NSA_CMP_STRIDE
    nsel = s // NSA_SEL_LEN
    cc = np.arange(s // NSA_CMP_STRIDE)
    cend = (ratio * (cc % nsel) + cc // nsel) * NSA_CMP_STRIDE + (NSA_CMP_LEN - 1)
    caug = np.zeros((cc.size, LANE), np.float32)
    caug[:, 0:3] = (cend % tk)[:, None]
    caug[:, 3:6] = (cend - cend % tk)[:, None]
    return (jnp.asarray(qbase.astype(bf)), jnp.asarray(kaug.astype(bf)), jnp.asarray(caug.astype(bf)))


def _nsa_kernel(q_ref, ks_ref, vs_ref, kw_ref, vw_ref, kaug_ref, kc_ref, vc_ref, caug_ref, qbase_ref,
                gate_ref, z_ref, y_ref, qp_ref, m_ref, l_ref, acc_ref, sc_ref, flag_ref, tiles_ref,
                *, tq, tk, nsel):
    i = pl.program_id(2)
    hg = NSA_HG
    r = hg * tq
    ratio = NSA_SEL_LEN // NSA_CMP_STRIDE
    ncb = nsel * ratio
    bpt = tk // NSA_SEL_LEN
    n_tiles = nsel // bpt
    halves_per_tile = NSA_IND_LANE // bpt
    t0 = i * tq

    qg = q_ref[0]
    qall = jnp.concatenate([qg[:, n * LANE:(n + 1) * LANE] for n in range(hg)], axis=0)
    base = qbase_ref[0]
    qwin = jnp.concatenate([qall, base], axis=1)
    trow = t0 + lax.broadcasted_iota(jnp.int32, (r, 1), 0) % tq
    last = (t0 + tq - 1) // tk
    dist0 = (lax.broadcasted_iota(jnp.int32, (r, tk), 0) % tq
             - lax.broadcasted_iota(jnp.int32, (r, tk), 1))

    def window_branch():
        n_win = NSA_WINDOW // tk + 1
        win_sc, win_v = [], []
        for j in range(n_win):
            kt = last - (n_win - 1) + j
            start = pl.multiple_of(jnp.maximum(kt, 0) * tk, tk)
            kp = jnp.concatenate([kw_ref[0, pl.ds(start, tk), :], kaug_ref[pl.ds(start, tk), :]], axis=1)
            off = kt * tk - t0
            if j == n_win - 1:
                mask = dist0 >= off
            elif j == 0:
                mask = dist0 < jnp.where(kt >= 0, off + NSA_WINDOW, -tk)
            else:
                mask = kt >= 0
            sc = jnp.where(mask, _dot_nt(qwin, kp), NEG)
            win_sc += [sc[:, c * LANE:(c + 1) * LANE] for c in range(tk // LANE)]
            win_v.append(vw_ref[0, pl.ds(start, tk), :])
        m_w = win_sc[0]
        for ch in win_sc[1:]:
            m_w = jnp.maximum(m_w, ch)
        m_w = jnp.max(m_w, axis=1, keepdims=True)
        win_p = [jnp.exp2(ch - m_w) for ch in win_sc]
        l_w = win_p[0]
        for x in win_p[1:]:
            l_w = l_w + x
        l_w = jnp.sum(l_w, axis=1, keepdims=True)
        cpt = tk // LANE
        o_w = _dot(jnp.concatenate([x.astype(BF16) for x in win_p[:cpt]], axis=1), win_v[0])
        for j in range(1, n_win):
            o_w = o_w + _dot(jnp.concatenate([x.astype(BF16) for x in win_p[j * cpt:(j + 1) * cpt]], axis=1),
                             win_v[j])
        return o_w / l_w

    cc = lax.broadcasted_iota(jnp.int32, (1, ncb), 1)
    cend = (ratio * (cc % nsel) + cc // nsel) * NSA_CMP_STRIDE + (NSA_CMP_LEN - 1)
    mask_c = trow >= cend
    kcp = jnp.concatenate([kc_ref[0, 0], caug_ref[...]], axis=1)
    s_c = jnp.where(mask_c, _dot_nt(qwin, kcp), NEG)
    m_c = jnp.max(s_c, axis=1, keepdims=True)
    e_c = jnp.exp2(s_c - m_c)
    l_c = jnp.sum(e_c, axis=1, keepdims=True)
    p_c = e_c * jnp.where(trow >= NSA_CMP_LEN - 1, 1.0 / l_c, 0.0)
    o_c = _dot(p_c.astype(BF16), vc_ref[0, 0])

    ph = p_c[0:tq]
    for n in range(1, hg):
        ph = ph + p_c[n * tq:(n + 1) * tq]
    imp = ph[:, 0:nsel]
    for a in range(1, ratio):
        imp = imp + ph[:, a * nsel:(a + 1) * nsel]
    if nsel < LANE:
        imp = jnp.concatenate([imp, jnp.zeros((tq, LANE - nsel), F32)], axis=1)

    imp_t = imp.T
    jj = lax.broadcasted_iota(jnp.int32, (LANE, tq), 0)
    cur = (t0 + lax.broadcasted_iota(jnp.int32, (1, tq), 1)) // NSA_SEL_LEN
    forced = (jj == 0) | (jj == cur) | (jj == cur - 1)
    taken = -(2.0 ** 127)
    val = jnp.where(forced, taken, jnp.where(jj <= cur, imp_t, -BIG))
    for _ in range(max(min(NSA_SEL_TOPK, nsel) - 3, 0)):
        vmax = jnp.max(val, axis=0, keepdims=True)
        first = jnp.min(jnp.where(val == vmax, jj, LANE), axis=0, keepdims=True)
        val = jnp.where(jj == first, taken, val)
    sel = jnp.where(val == taken, 1.0, 0.0).T

    o_w = window_branch()

    col_any = jnp.max(sel, axis=0, keepdims=True)
    for kt in range(n_tiles):
        flag_ref[kt] = (jnp.max(col_any[:, kt * bpt:(kt + 1) * bpt]) > 0.5).astype(jnp.int32)

    lane_q = lax.broadcasted_iota(jnp.int32, (tq, LANE), 1)
    lane_r = lax.broadcasted_iota(jnp.int32, (r, LANE), 1)
    unpicked = jnp.where(sel > 0.5, 0.0, -NSA_MASK)
    for half, part in enumerate((pltpu.roll(unpicked, NSA_IND_LANE, 1), unpicked)):
        part = jnp.where(lane_q >= NSA_IND_LANE, part, 0.0).astype(BF16)
        aug = jnp.where(lane_r >= NSA_IND_LANE, jnp.concatenate([part] * hg, axis=0), base)
        qp_ref[half] = jnp.concatenate([qall, aug], axis=1)

    m_ref[...] = jnp.full_like(m_ref, NEG)
    l_ref[...] = jnp.zeros_like(l_ref)
    acc_ref[...] = jnp.zeros_like(acc_ref)

    def compact(kt, n):
        tiles_ref[n] = kt
        return n + flag_ref[kt]

    n_act = lax.fori_loop(0, last, compact, 0)
    tiles_ref[n_act] = last

    def scores(kt):
        start = pl.multiple_of(kt * tk, tk)
        kp = jnp.concatenate([ks_ref[0, pl.ds(start, tk), :], kaug_ref[pl.ds(start, tk), :]], axis=1)
        return _dot_nt(qp_ref[kt // halves_per_tile], kp)

    def softmax_step(sc, kt):
        start = pl.multiple_of(kt * tk, tk)
        chunks = [sc[:, c * LANE:(c + 1) * LANE] for c in range(tk // LANE)]
        cmax = chunks[0]
        for ch in chunks[1:]:
            cmax = jnp.maximum(cmax, ch)
        m_old = m_ref[...]
        m_new = jnp.maximum(m_old, jnp.max(cmax, axis=1, keepdims=True))
        alpha = jnp.exp2(m_old - m_new)
        ps = [jnp.exp2(ch - m_new) for ch in chunks]
        psum = ps[0]
        for x in ps[1:]:
            psum = psum + x
        l_ref[...] = alpha * l_ref[...] + jnp.sum(psum, axis=1, keepdims=True)
        p = jnp.concatenate([x.astype(BF16) for x in ps], axis=1)
        acc_ref[...] = alpha * acc_ref[...] + _dot(p, vs_ref[0, pl.ds(start, tk), :])
        m_ref[...] = m_new

    sc_ref[0] = scores(tiles_ref[0])

    def step(j, slot):
        sc_ref[1 - slot] = scores(tiles_ref[j + 1])
        softmax_step(sc_ref[slot], tiles_ref[j])

    def pair_body(pj, c):
        step(2 * pj, 0)
        step(2 * pj + 1, 1)
        return c

    lax.fori_loop(0, n_act // 2, pair_body, 0)

    @pl.when(n_act % 2 == 1)
    def _():
        step(n_act - 1, 0)

    softmax_step(jnp.where(dist0 >= last * tk - t0, sc_ref[n_act % 2], NEG), last)
    o_s = acc_ref[...] / l_ref[...]

    gates = jax.nn.sigmoid(gate_ref[0])
    zg = z_ref[0]
    for n in range(hg):
        rs = slice(n * tq, (n + 1) * tq)
        o = (gates[:, n:n + 1] * o_c[rs]
             + gates[:, hg + n:hg + n + 1] * o_s[rs]
             + gates[:, 2 * hg + n:2 * hg + n + 1] * o_w[rs])
        cs = slice(n * LANE, (n + 1) * LANE)
        y_ref[0, :, cs] = (o * _silu(zg[:, cs])).astype(y_ref.dtype)


def _nsa_attention(pb, pf, kvc, b, s, *, tq=128, tk=256):
    g, hg = NSA_GROUPS, NSA_HG
    tq = min(tq, s)
    tk = min(tk, s)
    nsel = s // NSA_SEL_LEN
    ncb = s // NSA_CMP_STRIDE
    r = hg * tq
    assert s % tk == 0 and tk % tq == 0 and tk % LANE == 0 and nsel <= LANE and tk <= NSA_WINDOW
    assert NSA_IND_LANE % (tk // NSA_SEL_LEN) == 0 and tk <= 256
    qbase, kaug, caug = _nsa_constants(s, tq, tk)
    qblocks = NSA_HEADS
    zblocks = NSA_HEADS
    kv_spec = lambda off: pl.BlockSpec((1, s, LANE), lambda bi, gi, i: (bi, 0, qblocks + off * g + gi))
    return pl.pallas_call(
        functools.partial(_nsa_kernel, tq=tq, tk=tk, nsel=nsel),
        grid=(b, g, s // tq),
        in_specs=[
            pl.BlockSpec((1, tq, hg * LANE), lambda bi, gi, i: (bi, i, gi)),
            kv_spec(0), kv_spec(1), kv_spec(2), kv_spec(3),
            pl.BlockSpec((s, LANE), lambda bi, gi, i: (0, 0)),
            pl.BlockSpec((1, 1, ncb, LANE), lambda bi, gi, i: (bi, gi, 0, 0)),
            pl.BlockSpec((1, 1, ncb, LANE), lambda bi, gi, i: (bi, gi, 0, 0)),
            pl.BlockSpec((ncb, LANE), lambda bi, gi, i: (0, 0)),
            pl.BlockSpec((1, r, LANE), lambda bi, gi, i: (gi, 0, 0)),
            pl.BlockSpec((1, tq, LANE), lambda bi, gi, i: (bi, i, zblocks + gi)),
            pl.BlockSpec((1, tq, hg * LANE), lambda bi, gi, i: (bi, i, gi)),
        ],
        out_specs=pl.BlockSpec((1, tq, hg * LANE), lambda bi, gi, i: (bi, i, gi)),
        out_shape=jax.ShapeDtypeStruct((b, s, NSA_HEADS * LANE), BF16),
        scratch_shapes=[
            pltpu.VMEM((2, r, 2 * LANE), BF16),
            pltpu.VMEM((r, LANE), F32), pltpu.VMEM((r, LANE), F32), pltpu.VMEM((r, LANE), F32),
            pltpu.VMEM((2, r, tk), F32),
            pltpu.SMEM((nsel // (tk // NSA_SEL_LEN),), jnp.int32),
            pltpu.SMEM((nsel // (tk // NSA_SEL_LEN) + 1,), jnp.int32),
        ],
        compiler_params=pltpu.CompilerParams(
            dimension_semantics=("arbitrary", "arbitrary", "arbitrary"),
            vmem_limit_bytes=VMEM_LIMIT),
        name="nsa_attention",
    )(pb, pb, pb, pb, pb, kaug, kvc[0], kvc[1], caug, qbase, pf, pf)


def _gmlp_kernel(u_ref, v_ref, z_ref, vg_ref, ws_ref, bst_ref, y_ref, *, n_chunks):
    c = GM_CHUNK
    v = _gelu(v_ref[...])
    ms = jnp.mean(v * v, axis=-1, keepdims=True)
    vn = (v * lax.rsqrt(ms + EPS) * vg_ref[...]).astype(BF16)
    rows = lax.broadcasted_iota(jnp.int32, (c, c), 0)
    cols = lax.broadcasted_iota(jnp.int32, (c, c), 1)
    causal = cols <= rows
    bst = bst_ref[...]
    for gi in range(GM_GROUPS):
        ws = jnp.where(causal, ws_ref[gi], 0.0).astype(BF16)
        cs = slice(gi * LANE, (gi + 1) * LANE)
        for ci in range(n_chunks):
            rs = slice(ci * c, (ci + 1) * c)
            mixed = _dot(ws, vn[rs, cs]) + bst[:, gi:gi + 1]
            y = _gelu(u_ref[rs, cs]) * mixed * _silu(z_ref[rs, cs])
            y_ref[rs, cs] = y.astype(y_ref.dtype)


def _gmlp(pf, vg, ws, bs, *, n_chunks=2):
    n = pf.shape[0]
    w = pf.shape[1] // 3
    tm = GM_CHUNK * n_chunks
    assert n % tm == 0 and w == GM_GROUPS * LANE
    return pl.pallas_call(
        functools.partial(_gmlp_kernel, n_chunks=n_chunks),
        grid=(n // tm,),
        in_specs=[
            pl.BlockSpec((tm, w), lambda i: (i, 0)),
            pl.BlockSpec((tm, w), lambda i: (i, 1)),
            pl.BlockSpec((tm, w), lambda i: (i, 2)),
            pl.BlockSpec((1, w), lambda i: (0, 0)),
            pl.BlockSpec((GM_GROUPS, GM_CHUNK, GM_CHUNK), lambda i: (0, 0, 0)),
            pl.BlockSpec((GM_CHUNK, GM_GROUPS), lambda i: (0, 0)),
        ],
        out_specs=pl.BlockSpec((tm, w), lambda i: (i, 0)),
        out_shape=jax.ShapeDtypeStruct((n, w), BF16),
        compiler_params=pltpu.CompilerParams(
            dimension_semantics=("arbitrary",), vmem_limit_bytes=VMEM_LIMIT),
        name="gmlp_mix",
    )(pf, pf, pf, vg.reshape(1, w), ws, bs.T)


def _sb_layer(x2d, b, s, norm_g, w_in, j):
    width = w_in.shape[2] // 4
    scale = np.float32(LANE ** -0.5) * LOG2E
    col_scale = jnp.concatenate([jnp.full((width,), scale, F32), jnp.ones((3 * width,), F32)])
    qkv, zf = _norm_proj(x2d, norm_g, w_in.astype(BF16), j, col_scale, 3 * width, tn=1024)
    y = _sb_attention(qkv.reshape(b, s, 3 * width), zf.reshape(b, s, width), b, s)
    return y.reshape(b * s, width)


def _nsa_layer(x2d, b, s, norm_g, w_in, pos_k, pos_v, ck_w1, ck_w2, cv_w1, cv_w2):
    d = x2d.shape[1]
    h, g, hg = NSA_HEADS, NSA_GROUPS, NSA_HG
    kvw = g * LANE
    qw = h * LANE
    o_kc, o_vc, o_ks, o_vs, o_kw, o_vw = (qw + a * kvw for a in range(6))
    o_g = qw + 6 * kvw
    o_z = o_g + NSA_BRANCHES * h
    wg = w_in[:, o_g:o_z].reshape(d, NSA_BRANCHES, g, hg).transpose(0, 2, 1, 3)
    wg = wg.reshape(d, g, NSA_BRANCHES * hg)
    wg = jnp.pad(wg, ((0, 0), (0, 0), (0, LANE - NSA_BRANCHES * hg))).reshape(d, g * LANE)
    w_all = jnp.concatenate([
        w_in[:, :qw], w_in[:, o_ks:o_g],
        w_in[:, o_z:], wg, w_in[:, o_kc:o_ks],
    ], axis=1).astype(BF16)
    n1 = qw + 4 * kvw
    scale = np.float32(LANE ** -0.5) * LOG2E
    col_scale = jnp.concatenate([jnp.full((qw,), scale, F32), jnp.ones((w_all.shape[1] - qw,), F32)])
    pb, pf = _norm_proj(x2d, norm_g, w_all[None], 0, col_scale, n1)
    pf = pf.reshape(b, s, pf.shape[1])
    pb = pb.reshape(b, s, n1)

    pos = jnp.stack([pos_k, pos_v])
    w1 = jnp.stack([ck_w1, cv_w1]).astype(BF16)
    w2 = jnp.stack([ck_w2, cv_w2]).astype(BF16)
    kvc = _nsa_compress(pf, pos, w1, w2, b, s, (qw + g * LANE) // LANE)
    ratio = NSA_SEL_LEN // NSA_CMP_STRIDE
    nsel = s // NSA_SEL_LEN
    kvc = kvc.reshape(2, b, g, nsel, ratio, LANE).transpose(0, 1, 2, 4, 3, 5).reshape(2, b, g, nsel * ratio, LANE)

    y = _nsa_attention(pb, pf, kvc, b, s)
    return y.reshape(b * s, qw)


def _gmlp_layer(x2d, norm_g, w_in, j, v_norm_g, w_s, b_s):
    ntot = w_in.shape[2]
    _, pf = _norm_proj(x2d, norm_g, w_in.astype(BF16), j, jnp.ones((ntot,), F32), 0, tn=1024)
    return _gmlp(pf, v_norm_g, w_s, b_s)


def kernel(x, p, norm_g, final_norm_g, ple_proj, ple_gate, sb_w_in, sb_w_out, nsa_w_in, nsa_cmp_pos_k, nsa_cmp_pos_v, nsa_cmp_k_w1, nsa_cmp_k_w2, nsa_cmp_v_w1, nsa_cmp_v_w2, nsa_w_out, gm_w_in, gm_v_norm_g, gm_w_s, gm_b_s, gm_w_out):
    b, s, d = x.shape
    depth = p.shape[0]
    x2d = x.reshape(b * s, d)
    for i in range(depth):
        kind, j = i % 3, i // 3
        if kind == 0:
            y, wo = _sb_layer(x2d, b, s, norm_g[i], sb_w_in, j), sb_w_out
        elif kind == 1:
            y, wo = _nsa_layer(x2d, b, s, norm_g[i], nsa_w_in[j], nsa_cmp_pos_k[j], nsa_cmp_pos_v[j],
                               nsa_cmp_k_w1[j], nsa_cmp_k_w2[j], nsa_cmp_v_w1[j], nsa_cmp_v_w2[j]), nsa_w_out
        else:
            y, wo = _gmlp_layer(x2d, norm_g[i], gm_w_in, j, gm_v_norm_g[j], gm_w_s[j], gm_b_s[j]), gm_w_out
        x2d = _out_ple(y, x2d, p.reshape(depth, b * s, p.shape[-1]), i, wo.astype(BF16), j,
                       ple_gate.astype(BF16), ple_proj.astype(BF16), final_norm_g,
                       final=(i == depth - 1))
    return x2d.reshape(b, s, d)
```
